```python
import numpy as np
import jax, jax.numpy as jnp
from jax import lax

D_MODEL = 1024
BATCH = 8
SEQ = 2048
DEPTH = 4
DEC_BATCH = 128
DEC_SEQ = 1
PAST_LEN = 2048
PAGE_SIZE = 128

D_FF = 4 * D_MODEL
D_A = D_MODEL // 2
CONV_WIDTH = 31
D_B = D_MODEL // 2
B_GROUPS = 4
B_GROUP_DIM = D_B // B_GROUPS
CHUNK = 128
EVEN_IN = 2 * D_A + 2 * D_B
EVEN_OUT = D_A + D_B
N_HEADS = 16
HEAD_DIM = 64
N_KV = 4
GROUP = N_HEADS // N_KV
CMP_BLOCK = 32
CMP_STRIDE = 16
CMP_HIDDEN = 2 * HEAD_DIM
SEL_BLOCK = 64
N_SEL = 8
WINDOW = 512
Q_BLOCK = 64
ODD_IN = N_HEADS * HEAD_DIM + 6 * N_KV * HEAD_DIM + 3 * N_HEADS
ODD_OUT = N_HEADS * HEAD_DIM
ROPE_THETA = 10000.0
EPS = 1e-6
NEG = -1e30
FORCED = 1e4

kernel_name = "hybrid_conv_sgu_nsa_decoder_step"


def rmsnorm(x, g):
    xf = x.astype(jnp.float32)
    y = xf * lax.rsqrt(jnp.mean(xf * xf, -1, keepdims=True) + EPS)
    return (y * g.astype(jnp.float32)).astype(x.dtype)


def layernorm(x, g, b):
    xf = x.astype(jnp.float32)
    xc = xf - jnp.mean(xf, -1, keepdims=True)
    y = xc * lax.rsqrt(jnp.mean(xc * xc, -1, keepdims=True) + EPS)
    return (y * g.astype(jnp.float32) + b.astype(jnp.float32)).astype(x.dtype)


def rope(x, pos):
    half = HEAD_DIM // 2
    inv = jnp.power(jnp.float32(ROPE_THETA), -jnp.arange(half, dtype=jnp.float32) * (2.0 / HEAD_DIM))
    ang = pos.astype(jnp.float32)[:, None] * inv[None, :]
    cos = jnp.cos(ang)[None, :, None, :]
    sin = jnp.sin(ang)[None, :, None, :]
    xf = x.astype(jnp.float32)
    x1, x2 = xf[..., :half], xf[..., half:]
    return jnp.concatenate([x1 * cos - x2 * sin, x2 * cos + x1 * sin], -1).astype(x.dtype)


def masked_softmax(s, mask):
    s = jnp.where(mask, s.astype(jnp.float32), NEG)
    m = jnp.max(s, -1, keepdims=True)
    p = jnp.where(mask, jnp.exp(s - m), 0.0)
    return p / jnp.maximum(jnp.sum(p, -1, keepdims=True), 1e-30)


def ada_params(c, w, b):
    m = jax.nn.silu(c) @ w + b
    return [t[:, None, :] for t in jnp.split(m, 6, axis=-1)]


def modulate(x, g, shift, scale):
    return rmsnorm(x, g) * (1 + scale) + shift


def sq_relu_mlp(h, w1, w2):
    return jnp.square(jax.nn.relu(h @ w1)) @ w2


def causal_depthwise(a, buf, w, b):
    ap = jnp.concatenate([buf.astype(a.dtype), a], axis=1)
    out = lax.conv_general_dilated(ap, w[:, None, :], window_strides=(1,), padding='VALID',
                                   dimension_numbers=('NWC', 'WIO', 'NWC'),
                                   feature_group_count=a.shape[-1])
    return out + b, ap[:, -(CONV_WIDTH - 1):]


def chunk_spatial_gate(v, w_s, b_s):
    Bn, L, _ = v.shape
    nc = -(-L // CHUNK)
    vp = jnp.pad(v, ((0, 0), (0, nc * CHUNK - L), (0, 0))).reshape(Bn, nc, CHUNK, B_GROUPS, B_GROUP_DIM)
    causal = jnp.tril(jnp.ones((CHUNK, CHUNK), dtype=bool))
    wm = jnp.where(causal[None], w_s, jnp.zeros((), w_s.dtype))
    out = jnp.einsum('gts,bcsgd->bctgd', wm, vp) + b_s.T[None, None, :, :, None]
    return out.reshape(Bn, nc * CHUNK, D_B)[:, :L]


def even_mixer(h, conv_buf, w_in, w_out, conv_w, conv_b, conv_ln_g, conv_ln_b,
               sgu_ln_g, sgu_ln_b, sgu_w, sgu_b):
    z = h @ w_in
    a_lin, a_gate, b_u, b_v = jnp.split(z, [D_A, 2 * D_A, 2 * D_A + D_B], axis=-1)
    a = a_lin * jax.nn.sigmoid(a_gate)
    a_conv, new_buf = causal_depthwise(a, conv_buf, conv_w, conv_b)
    a_out = jax.nn.silu(layernorm(a_conv, conv_ln_g, conv_ln_b))
    u = jax.nn.gelu(b_u)
    v = layernorm(jax.nn.gelu(b_v), sgu_ln_g, sgu_ln_b)
    b_out = u * chunk_spatial_gate(v, sgu_w, sgu_b)
    y = jnp.concatenate([a_out, b_out], -1) @ w_out
    L = h.shape[1]
    cur = ((L - 1) // CHUNK) * CHUNK
    return y, new_buf, v[:, cur:]


def nsa_project(h, w_in, pos):
    z = h @ w_in
    kv = N_KV * HEAD_DIM
    q0 = N_HEADS * HEAD_DIM
    splits = [q0 + i * kv for i in range(7)]
    q, ck, cv, sk, sv, wk, wv, g = jnp.split(z, splits, axis=-1)
    Bn, L, _ = h.shape
    kvr = lambda t: t.reshape(Bn, L, N_KV, HEAD_DIM)
    q = q.reshape(Bn, L, N_HEADS, HEAD_DIM)
    q_rot = rope(q, pos)
    gates = jax.nn.sigmoid(g.reshape(Bn, L, N_HEADS, 3))
    return q, q_rot, kvr(ck), kvr(cv), rope(kvr(sk), pos), kvr(sv), rope(kvr(wk), pos), kvr(wv), gates


def compress(k, pe, w1, w2):
    Bn, L = k.shape[:2]
    r = CMP_BLOCK // CMP_STRIDE
    n_cmp = (L - CMP_BLOCK) // CMP_STRIDE + 1
    n_chunk = n_cmp + r - 1
    ch = k[:, :n_chunk * CMP_STRIDE].reshape(Bn, n_chunk, CMP_STRIDE, N_KV, HEAD_DIM)
    w1r = w1.reshape(r, CMP_STRIDE, HEAD_DIM, CMP_HIDDEN)
    hid = jnp.einsum('ld,lde->e', pe, w1)
    for m in range(r):
        hid = hid + jnp.einsum('bcjkd,jde->bcke', ch[:, m:m + n_cmp], w1r[m])
    return jax.nn.gelu(hid) @ w2


def cmp_last_pos(n_cmp):
    return jnp.arange(n_cmp) * CMP_STRIDE + (CMP_BLOCK - 1)


def to_sel_blocks(k):
    Bn, L = k.shape[:2]
    ns = -(-L // SEL_BLOCK)
    kp = jnp.pad(k, ((0, 0), (0, ns * SEL_BLOCK - L), (0, 0), (0, 0)))
    return kp.reshape(Bn, ns, SEL_BLOCK, N_KV, HEAD_DIM).transpose(0, 3, 1, 2, 4)


def overlap_matrix(n_cmp, n_sel):
    i = np.arange(n_cmp)[:, None] * CMP_STRIDE
    j = np.arange(n_sel)[None, :] * SEL_BLOCK
    return ((i < j + SEL_BLOCK) & (i + CMP_BLOCK > j)).astype(np.float32)


def nsa_attend(q, q_rot, q_pos, gates, k_cmp, v_cmp, cmp_last, sel_k, sel_v, k_win, v_win, win_pos):
    Bn, T = q.shape[:2]
    scale = HEAD_DIM ** -0.5
    qg = q.reshape(Bn, T, N_KV, GROUP, HEAD_DIM)
    qr = q_rot.reshape(Bn, T, N_KV, GROUP, HEAD_DIM)
    s = jnp.einsum('btkgd,bnkd->btkgn', qg, k_cmp) * scale
    cmask = (cmp_last[None, :] <= q_pos[:, None])[None, :, None, None, :]
    p_cmp = masked_softmax(s, cmask)
    o_cmp = jnp.einsum('btkgn,bnkd->btkgd', p_cmp.astype(v_cmp.dtype), v_cmp)
    n_cmp, n_sel = k_cmp.shape[1], sel_k.shape[2]
    ov = jnp.asarray(overlap_matrix(n_cmp, n_sel))
    imp = jnp.einsum('btkn,nj->btkj', p_cmp.sum(3), ov)
    blk = jnp.arange(n_sel)[None, :]
    cur = (q_pos // SEL_BLOCK)[:, None]
    valid = blk * SEL_BLOCK <= q_pos[:, None]
    forced = (blk == 0) | (blk == cur) | (blk == cur - 1)
    score = jnp.where(valid[None, :, None, :], jnp.where(forced[None, :, None, :], FORCED, imp), NEG)
    n_top = min(N_SEL, n_sel)
    _, top_idx = lax.top_k(score, n_top)
    bi = jnp.arange(Bn)[:, None, None, None]
    ki = jnp.arange(N_KV)[None, None, :, None]
    ks = sel_k[bi, ki, top_idx].reshape(Bn, T, N_KV, n_top * SEL_BLOCK, HEAD_DIM)
    vs = sel_v[bi, ki, top_idx].reshape(Bn, T, N_KV, n_top * SEL_BLOCK, HEAD_DIM)
    kpos = (top_idx[..., None] * SEL_BLOCK + jnp.arange(SEL_BLOCK)).reshape(Bn, T, N_KV, n_top * SEL_BLOCK)
    smask = (kpos <= q_pos[None, :, None, None])[:, :, :, None, :]
    s = jnp.einsum('btkgd,btksd->btkgs', qr, ks) * scale
    p = masked_softmax(s, smask)
    o_sel = jnp.einsum('btkgs,btksd->btkgd', p.astype(vs.dtype), vs)
    d = q_pos[:, None] - win_pos[None, :]
    wmask = ((d >= 0) & (d <= WINDOW) & (win_pos[None, :] >= 0))[None, :, None, None, :]
    s = jnp.einsum('btkgd,bwkd->btkgw', qr, k_win) * scale
    p = masked_softmax(s, wmask)
    o_win = jnp.einsum('btkgw,bwkd->btkgd', p.astype(v_win.dtype), v_win)
    g = gates.reshape(Bn, T, N_KV, GROUP, 3).astype(o_cmp.dtype)
    o = g[..., 0:1] * o_cmp + g[..., 1:2] * o_sel + g[..., 2:3] * o_win
    return o.reshape(Bn, T, N_HEADS * HEAD_DIM)


def nsa_prompt(h, w_in, w_out, pe_k, w1_k, w2_k, pe_v, w1_v, w2_v):
    Bn, L, _ = h.shape
    pos = jnp.arange(L)
    q, q_rot, ck, cv, sk, sv, wk, wv, gates = nsa_project(h, w_in, pos)
    k_cmp = compress(ck, pe_k, w1_k, w2_k)
    v_cmp = compress(cv, pe_v, w1_v, w2_v)
    cmp_last = cmp_last_pos(k_cmp.shape[1])
    sel_k, sel_v = to_sel_blocks(sk), to_sel_blocks(sv)
    wk_p = jnp.pad(wk, ((0, 0), (WINDOW, 0), (0, 0), (0, 0)))
    wv_p = jnp.pad(wv, ((0, 0), (WINDOW, 0), (0, 0), (0, 0)))
    nq = L // Q_BLOCK

    def blocks(t):
        return t.reshape(Bn, nq, Q_BLOCK, *t.shape[2:]).swapaxes(0, 1)

    def body(xs):
        qb, qrb, gb, start = xs
        qpos = start + jnp.arange(Q_BLOCK)
        kw = lax.dynamic_slice_in_dim(wk_p, start, WINDOW + Q_BLOCK, axis=1)
        vw = lax.dynamic_slice_in_dim(wv_p, start, WINDOW + Q_BLOCK, axis=1)
        wpos = start - WINDOW + jnp.arange(WINDOW + Q_BLOCK)
        return nsa_attend(qb, qrb, qpos, gb, k_cmp, v_cmp, cmp_last, sel_k, sel_v, kw, vw, wpos)

    o = lax.map(body, (blocks(q), blocks(q_rot), blocks(gates), jnp.arange(nq) * Q_BLOCK))
    o = o.swapaxes(0, 1).reshape(Bn, L, N_HEADS * HEAD_DIM)
    nw = min(WINDOW, L)
    return o @ w_out, (ck, cv, sk, sv, wk[:, L - nw:], wv[:, L - nw:])


def nsa_sample(h, cmp_k_pool, cmp_v_pool, sel_k_pool, sel_v_pool, win_k, win_v, page_table, layer,
               w_in, w_out, pe_k, w1_k, w2_k, pe_v, w1_v, w2_v):
    Bn, T, _ = h.shape
    past = page_table.shape[1] * PAGE_SIZE
    pos = past + jnp.arange(T)
    q, q_rot, ck, cv, sk, sv, wk, wv, gates = nsa_project(h, w_in, pos)

    def paged(pool, new):
        rows = pool[layer, page_table].reshape(Bn, past, N_KV, HEAD_DIM)
        return jnp.concatenate([rows.astype(new.dtype), new], axis=1)

    k_cmp = compress(paged(cmp_k_pool, ck), pe_k, w1_k, w2_k)
    v_cmp = compress(paged(cmp_v_pool, cv), pe_v, w1_v, w2_v)
    cmp_last = cmp_last_pos(k_cmp.shape[1])
    sel_k = to_sel_blocks(paged(sel_k_pool, sk))
    sel_v = to_sel_blocks(paged(sel_v_pool, sv))
    nbuf = win_k.shape[1]
    kw = jnp.concatenate([win_k.astype(wk.dtype), wk], axis=1)
    vw = jnp.concatenate([win_v.astype(wv.dtype), wv], axis=1)
    wpos = past - nbuf + jnp.arange(nbuf + T)
    o = nsa_attend(q, q_rot, pos, gates, k_cmp, v_cmp, cmp_last, sel_k, sel_v, kw, vw, wpos)
    nw = min(WINDOW, past + T)
    return o @ w_out, (ck, cv, sk, sv, kw[:, -nw:], vw[:, -nw:])


def setup_inputs(seed: int = 0) -> dict:
    key = jax.random.key(seed)
    ks = iter(jax.random.split(key, 48))
    n_even = (DEPTH + 1) // 2
    n_odd = DEPTH // 2
    n_pages = PAST_LEN // PAGE_SIZE
    n_used = DEC_BATCH * n_pages
    n_pool = n_used + n_used // 4
    win_buf = min(WINDOW, PAST_LEN)

    def nrm(shape, s):
        return jax.random.normal(next(ks), shape, jnp.float32) * s

    page_table = jax.random.permutation(next(ks), n_pool)[:n_used].reshape(DEC_BATCH, n_pages).astype(jnp.int32)
    pool_shape = (n_odd, n_pool, PAGE_SIZE, N_KV, HEAD_DIM)
    win_shape = (n_odd, DEC_BATCH, win_buf, N_KV, HEAD_DIM)
    return dict(
        x_prompt=nrm((BATCH, SEQ, D_MODEL), 1.0),
        x_sample=nrm((DEC_BATCH, DEC_SEQ, D_MODEL), 1.0),
        state_conv=nrm((n_even, DEC_BATCH, CONV_WIDTH - 1, D_A), 0.5),
        cache_cmp_k=nrm(pool_shape, 1.0),
        cache_cmp_v=nrm(pool_shape, 1.0),
        cache_sel_k=nrm(pool_shape, 1.0),
        cache_sel_v=nrm(pool_shape, 1.0),
        state_win_k=nrm(win_shape, 1.0),
        state_win_v=nrm(win_shape, 1.0),
        page_table=page_table,
        c_prompt=nrm((BATCH, D_MODEL), 1.0),
        c_sample=nrm((DEC_BATCH, D_MODEL), 1.0),
        ada_w=nrm((DEPTH, D_MODEL, 6 * D_MODEL), 0.5 * D_MODEL ** -0.5),
        ada_b=nrm((DEPTH, 6 * D_MODEL), 0.02),
        norm_mix_g=1.0 + nrm((DEPTH, D_MODEL), 0.02),
        norm_ffn_g=1.0 + nrm((DEPTH, D_MODEL), 0.02),
        ffn_w1=nrm((DEPTH, D_MODEL, D_FF), D_MODEL ** -0.5),
        ffn_w2=nrm((DEPTH, D_FF, D_MODEL), D_FF ** -0.5),
        even_w_in=nrm((n_even, D_MODEL, EVEN_IN), D_MODEL ** -0.5),
        even_w_out=nrm((n_even, EVEN_OUT, D_MODEL), EVEN_OUT ** -0.5),
        conv_w=nrm((n_even, CONV_WIDTH, D_A), CONV_WIDTH ** -0.5),
        conv_b=nrm((n_even, D_A), 0.02),
        conv_ln_g=1.0 + nrm((n_even, D_A), 0.02),
        conv_ln_b=nrm((n_even, D_A), 0.02),
        sgu_ln_g=1.0 + nrm((n_even, D_B), 0.02),
        sgu_ln_b=nrm((n_even, D_B), 0.02),
        sgu_w=nrm((n_even, B_GROUPS, CHUNK, CHUNK), CHUNK ** -0.5),
        sgu_b=1.0 + nrm((n_even, B_GROUPS, CHUNK), 0.02),
        odd_w_in=nrm((n_odd, D_MODEL, ODD_IN), D_MODEL ** -0.5),
        odd_w_out=nrm((n_odd, ODD_OUT, D_MODEL), ODD_OUT ** -0.5),
        cmp_pe_k=nrm((n_odd, CMP_BLOCK, HEAD_DIM), 0.1),
        cmp_w1_k=nrm((n_odd, CMP_BLOCK, HEAD_DIM, CMP_HIDDEN), (CMP_BLOCK * HEAD_DIM) ** -0.5),
        cmp_w2_k=nrm((n_odd, CMP_HIDDEN, HEAD_DIM), CMP_HIDDEN ** -0.5),
        cmp_pe_v=nrm((n_odd, CMP_BLOCK, HEAD_DIM), 0.1),
        cmp_w1_v=nrm((n_odd, CMP_BLOCK, HEAD_DIM, CMP_HIDDEN), (CMP_BLOCK * HEAD_DIM) ** -0.5),
        cmp_w2_v=nrm((n_odd, CMP_HIDDEN, HEAD_DIM), CMP_HIDDEN ** -0.5),
        final_norm_g=1.0 + nrm((D_MODEL,), 0.02),
    )


def reference(x_prompt, x_sample, state_conv, cache_cmp_k, cache_cmp_v, cache_sel_k, cache_sel_v,
              state_win_k, state_win_v, page_table, c_prompt, c_sample, ada_w, ada_b, norm_mix_g,
              norm_ffn_g, ffn_w1, ffn_w2, even_w_in, even_w_out, conv_w, conv_b, conv_ln_g, conv_ln_b,
              sgu_ln_g, sgu_ln_b, sgu_w, sgu_b, odd_w_in, odd_w_out, cmp_pe_k, cmp_w1_k, cmp_w2_k,
              cmp_pe_v, cmp_w1_v, cmp_w2_v, final_norm_g):
    xp, xs = x_prompt, x_sample
    conv_p, conv_s, chv_p, chv_s = [], [], [], []
    nsa_p = [[] for _ in range(6)]
    nsa_s = [[] for _ in range(6)]
    for l in range(DEPTH):
        mp = ada_params(c_prompt, ada_w[l], ada_b[l])
        ms = ada_params(c_sample, ada_w[l], ada_b[l])
        hp = modulate(xp, norm_mix_g[l], mp[0], mp[1])
        hs = modulate(xs, norm_mix_g[l], ms[0], ms[1])
        if l % 2 == 0:
            e = l // 2
            ew = (even_w_in[e], even_w_out[e], conv_w[e], conv_b[e], conv_ln_g[e], conv_ln_b[e],
                  sgu_ln_g[e], sgu_ln_b[e], sgu_w[e], sgu_b[e])
            zero_buf = jnp.zeros((hp.shape[0], CONV_WIDTH - 1, D_A), hp.dtype)
            yp, bp, vp = even_mixer(hp, zero_buf, *ew)
            ys, bs, vs = even_mixer(hs, state_conv[e], *ew)
            conv_p.append(bp); conv_s.append(bs); chv_p.append(vp); chv_s.append(vs)
        else:
            o = l // 2
            ow = (odd_w_in[o], odd_w_out[o], cmp_pe_k[o], cmp_w1_k[o], cmp_w2_k[o],
                  cmp_pe_v[o], cmp_w1_v[o], cmp_w2_v[o])
            yp, newp = nsa_prompt(hp, *ow)
            ys, news = nsa_sample(hs, cache_cmp_k, cache_cmp_v, cache_sel_k, cache_sel_v,
                                  state_win_k[o], state_win_v[o], page_table, o, *ow)
            for i in range(6):
                nsa_p[i].append(newp[i]); nsa_s[i].append(news[i])
        xp = xp + mp[2] * yp
        xs = xs + ms[2] * ys
        xp = xp + mp[5] * sq_relu_mlp(modulate(xp, norm_ffn_g[l], mp[3], mp[4]), ffn_w1[l], ffn_w2[l])
        xs = xs + ms[5] * sq_relu_mlp(modulate(xs, norm_ffn_g[l], ms[3], ms[4]), ffn_w1[l], ffn_w2[l])
    y_prompt = rmsnorm(xp, final_norm_g)
    y_sample = rmsnorm(xs, final_norm_g)
    st = lambda lst: jnp.stack(lst, axis=0)
    return (y_prompt, y_sample,
            st(conv_p), st(conv_s), st(chv_p), st(chv_s),
            st(nsa_p[0]), st(nsa_p[1]), st(nsa_p[2]), st(nsa_p[3]), st(nsa_p[4]), st(nsa_p[5]),
            st(nsa_s[0]), st(nsa_s[1]), st(nsa_s[2]), st(nsa_s[3]), st(nsa_s[4]), st(nsa_s[5]))
```

```python
import functools

import numpy as np
import jax
import jax.numpy as jnp
from jax import lax
from jax.experimental import pallas as pl
from jax.experimental.pallas import tpu as pltpu

F32 = jnp.float32
BF16 = jnp.bfloat16
HIGHEST = lax.Precision.HIGHEST

LANES = 128
SUBLANES = 8
VMEM_LIMIT_BYTES = 56 * 1024 * 1024

D_MODEL = 1024
D_FF = 4 * D_MODEL
D_A = D_MODEL // 2
CONV_WIDTH = 31
CONV_HALO = 32
D_B = D_MODEL // 2
B_GROUPS = 4
B_GROUP_DIM = D_B // B_GROUPS
CHUNK = 128
N_HEADS = 16
HEAD_DIM = 64
N_KV = 4
GROUP = N_HEADS // N_KV
KV_DIM = N_KV * HEAD_DIM
N_PAIR = KV_DIM // LANES
CMP_BLOCK = 32
CMP_STRIDE = 16
CMP_HIDDEN = 2 * HEAD_DIM
SEL_BLOCK = 64
N_SEL = 8
SEL_LANES = 32
SEL_SHIFT = 6
SEL_LANES_SHIFT = 5
WINDOW = 512
PAGE_SIZE = 128
ROPE_THETA = 10000.0
EPS = 1e-6
NEG = -1e30
FORCED = 1e4
SCALE = HEAD_DIM ** -0.5
ODD_IN_PAD = N_HEADS * HEAD_DIM + 6 * KV_DIM + LANES


def _cparams(*sem):
    return pltpu.CompilerParams(dimension_semantics=sem, vmem_limit_bytes=VMEM_LIMIT_BYTES)


def _const_spec(shape):
    n = len(shape)
    return pl.BlockSpec(shape, lambda *_: (0,) * n)


def _sigmoid(x):
    return 1.0 / (1.0 + jnp.exp(-x))


def _silu(x):
    return x * _sigmoid(x)


def _gelu(x):
    return 0.5 * x * (1.0 + jnp.tanh(np.sqrt(2.0 / np.pi).astype(np.float32) * (x + 0.044715 * (x * x * x))))


def _rmsnorm(x, g):
    return x * lax.rsqrt(jnp.mean(x * x, -1, keepdims=True) + EPS) * g


def _layernorm(x, g, b):
    xc = x - jnp.mean(x, -1, keepdims=True)
    return xc * lax.rsqrt(jnp.mean(xc * xc, -1, keepdims=True) + EPS) * g + b


def _modulate(x, g, shift, scale):
    return _rmsnorm(x, g) * (1.0 + scale) + shift


def _dot(a, b):
    return jnp.dot(a, b, preferred_element_type=F32)


def _dot_nt(a, b):
    return lax.dot_general(a, b, (((1,), (1,)), ((), ())), preferred_element_type=F32)


def _lane_iota(shape):
    return lax.broadcasted_iota(jnp.int32, shape, len(shape) - 1)


def _masked_softmax(s, mask):
    s = jnp.where(mask, s, NEG)
    m = jnp.max(s, -1, keepdims=True)
    p = jnp.where(mask, jnp.exp(s - m), 0.0)
    return p / jnp.maximum(jnp.sum(p, -1, keepdims=True), 1e-30)


def _rope_pair(xs, cos, sin_signed):
    half = HEAD_DIM // 2
    first = (_lane_iota(xs.shape) & (HEAD_DIM - 1)) < half
    swapped = jnp.where(first, pltpu.roll(xs, LANES - half, 1), pltpu.roll(xs, half, 1))
    return xs * cos + swapped * sin_signed


def _stack_group_heads(q_ref, k, rows):
    par = k % 2
    lane = _lane_iota((rows, LANES))
    keep = (lane < HEAD_DIM) if par == 0 else (lane >= HEAD_DIM)
    parts = []
    for g in range(GROUP):
        col = (GROUP * k + g) // 2
        xs = q_ref[0, :, col * LANES:(col + 1) * LANES]
        if g % 2 != par:
            xs = pltpu.roll(xs, HEAD_DIM, 1)
        parts.append(jnp.where(keep, xs, 0.0))
    return jnp.concatenate(parts, axis=0).astype(BF16)


def _select_blocks(imp, pos, n_top):
    lane = _lane_iota(imp.shape)
    j = lane & (SEL_LANES - 1)
    cur = pos >> SEL_SHIFT
    valid = j * SEL_BLOCK <= pos
    forced = (j == 0) | (j == cur) | (j == cur - 1)
    score = jnp.where(valid, jnp.where(forced, FORCED, imp), NEG)
    rank = jnp.zeros(imp.shape, F32)
    for d in range(1, SEL_LANES):
        nonwrap = j >= d
        rot = jnp.where(nonwrap, pltpu.roll(score, d, 1), pltpu.roll(score, (d - SEL_LANES) % LANES, 1))
        ahead = (rot > score) | ((rot == score) & nonwrap)
        rank = rank + jnp.where(ahead, 1.0, 0.0)
    return jnp.where(rank < n_top, 1.0, 0.0)


def _block_expand(k_sel, key0, n_keys):
    r = lax.broadcasted_iota(jnp.int32, (LANES, n_keys), 0)
    c = lax.broadcasted_iota(jnp.int32, (LANES, n_keys), 1)
    hit = (r & (SEL_LANES - 1)) == ((key0 + c) >> SEL_SHIFT)
    if k_sel is not None:
        hit = hit & ((r >> SEL_LANES_SHIFT) == k_sel)
    return jnp.where(hit, 1.0, 0.0).astype(BF16)


def _ada_body(c_ref, w_ref, b_ref, o_ref):
    a = _silu(c_ref[...])
    o_ref[0, 0] = jnp.dot(a, w_ref[0], preferred_element_type=F32, precision=HIGHEST) + b_ref[0, 0]


def _ada_all(c_all, ada_w, ada_b):
    depth = ada_w.shape[0]
    rows = c_all.shape[0]
    return pl.pallas_call(
        _ada_body,
        out_shape=jax.ShapeDtypeStruct((depth, 6, rows, D_MODEL), F32),
        grid=(depth, 6),
        in_specs=[
            _const_spec((rows, D_MODEL)),
            pl.BlockSpec((1, D_MODEL, D_MODEL), lambda l, j: (l, 0, j)),
            pl.BlockSpec((1, 1, 1, D_MODEL), lambda l, j: (l, j, 0, 0)),
        ],
        out_specs=pl.BlockSpec((1, 1, rows, D_MODEL), lambda l, j: (l, j, 0, 0)),
        compiler_params=_cparams("arbitrary", "arbitrary"),
        name="ada_params",
    )(c_all, ada_w, ada_b.reshape(depth, 6, 1, D_MODEL))


def _post_body(x_ref, y_ref, mod_ref, gffn_ref, wo_ref, w1_ref, w2_ref, gfin_ref, o_ref, *, final, ff_chunk):
    x = x_ref[0]
    x = x + mod_ref[2, 0] * _dot(y_ref[0], wo_ref[...])
    h = _modulate(x, gffn_ref[...], mod_ref[3, 0], mod_ref[4, 0]).astype(BF16)
    acc = jnp.zeros(x.shape, F32)
    for c in range(D_FF // ff_chunk):
        t = _dot(h, w1_ref[:, c * ff_chunk:(c + 1) * ff_chunk])
        t = jnp.square(jnp.maximum(t, 0.0)).astype(BF16)
        acc = acc + _dot(t, w2_ref[c * ff_chunk:(c + 1) * ff_chunk, :])
    x = x + mod_ref[5, 0] * acc
    if final:
        x = _rmsnorm(x, gfin_ref[...])
    o_ref[0] = x


def _post(x, y, mods, g_ffn, w_out, w1, w2, g_fin, *, final, tm):
    bn, ln, _ = x.shape
    d_in = y.shape[-1]
    r = mods.shape[2]
    tm = min(tm, ln)
    mod_rows = 1 if r == 1 else tm
    mod_map = (lambda b, t: (0, b, 0, 0)) if r == 1 else (lambda b, t: (0, b, t, 0))
    return pl.pallas_call(
        functools.partial(_post_body, final=final, ff_chunk=1024),
        out_shape=jax.ShapeDtypeStruct(x.shape, F32),
        grid=(bn, ln // tm),
        in_specs=[
            pl.BlockSpec((1, tm, D_MODEL), lambda b, t: (b, t, 0)),
            pl.BlockSpec((1, tm, d_in), lambda b, t: (b, t, 0)),
            pl.BlockSpec((6, 1, mod_rows, D_MODEL), mod_map),
            _const_spec((1, D_MODEL)),
            _const_spec((d_in, D_MODEL)),
            _const_spec((D_MODEL, D_FF)),
            _const_spec((D_FF, D_MODEL)),
            _const_spec((1, D_MODEL)),
        ],
        out_specs=pl.BlockSpec((1, tm, D_MODEL), lambda b, t: (b, t, 0)),
        compiler_params=_cparams("arbitrary", "arbitrary"),
        name="post_mlp",
    )(x, y, mods, g_ffn, w_out, w1, w2, g_fin)


def _even_prompt_body(x_ref, mod_ref, gmix_ref, win_ref, cw_ref, cb_ref, clg_ref, clb_ref, slg_ref, slb_ref,
                      sw_ref, sbt_ref, y_ref, conv_ref, chv_ref, ext_ref, *, tm, conv_rows):
    t = pl.program_id(1)
    h = _modulate(x_ref[0], gmix_ref[...], mod_ref[0, 0], mod_ref[1, 0]).astype(BF16)
    z = _dot(h, win_ref[...])
    a = z[:, :D_A] * _sigmoid(z[:, D_A:2 * D_A])
    u = _gelu(z[:, 2 * D_A:2 * D_A + D_B])
    v = _layernorm(_gelu(z[:, 2 * D_A + D_B:]), slg_ref[...], slb_ref[...])

    @pl.when(t == 0)
    def _():
        ext_ref[0:CONV_HALO, :] = jnp.zeros((CONV_HALO, D_A), F32)

    ext_ref[CONV_HALO:CONV_HALO + tm, :] = a
    first = CONV_HALO - (CONV_WIDTH - 1)
    pieces = []
    for c in range(tm // conv_rows):
        acc = jnp.zeros((conv_rows, D_A), F32) + cb_ref[...]
        for w in range(CONV_WIDTH):
            r0 = first + w + c * conv_rows
            acc = acc + ext_ref[r0:r0 + conv_rows, :] * cw_ref[w:w + 1, :]
        pieces.append(acc)
    a_conv = jnp.concatenate(pieces, axis=0)
    conv_ref[0] = ext_ref[tm + first:tm + CONV_HALO, :]
    ext_ref[0:CONV_HALO, :] = ext_ref[tm:tm + CONV_HALO, :]
    a_out = _silu(_layernorm(a_conv, clg_ref[...], clb_ref[...]))

    row = lax.broadcasted_iota(jnp.int32, (CHUNK, CHUNK), 0)
    col = lax.broadcasted_iota(jnp.int32, (CHUNK, CHUNK), 1)
    wm = [jnp.where(col <= row, sw_ref[g], 0.0).astype(BF16) for g in range(B_GROUPS)]
    vb = v.astype(BF16)
    gate_rows = []
    for c in range(tm // CHUNK):
        cols = []
        for g in range(B_GROUPS):
            vg = vb[c * CHUNK:(c + 1) * CHUNK, g * B_GROUP_DIM:(g + 1) * B_GROUP_DIM]
            cols.append(_dot(wm[g], vg) + sbt_ref[:, g:g + 1])
        gate_rows.append(jnp.concatenate(cols, axis=1))
    b_out = u * jnp.concatenate(gate_rows, axis=0)

    y_ref[0, :, :D_A] = a_out.astype(BF16)
    y_ref[0, :, D_A:] = b_out.astype(BF16)
    chv_ref[0] = v[tm - CHUNK:, :]


def _even_prompt(x, mods, g_mix, w_in, conv_w, conv_b, cln_g, cln_b, sln_g, sln_b, sgu_w, sgu_bt, *, tm):
    bn, ln, _ = x.shape
    assert ln % tm == 0 and tm % CHUNK == 0 and ln >= CONV_HALO
    return pl.pallas_call(
        functools.partial(_even_prompt_body, tm=tm, conv_rows=64),
        out_shape=(
            jax.ShapeDtypeStruct((bn, ln, D_MODEL), BF16),
            jax.ShapeDtypeStruct((bn, CONV_WIDTH - 1, D_A), F32),
            jax.ShapeDtypeStruct((bn, CHUNK, D_B), F32),
        ),
        grid=(bn, ln // tm),
        in_specs=[
            pl.BlockSpec((1, tm, D_MODEL), lambda b, t: (b, t, 0)),
            pl.BlockSpec((6, 1, 1, D_MODEL), lambda b, t: (0, b, 0, 0)),
            _const_spec((1, D_MODEL)),
            _const_spec((D_MODEL, 2 * D_A + 2 * D_B)),
            _const_spec((CONV_WIDTH, D_A)),
            _const_spec((1, D_A)), _const_spec((1, D_A)), _const_spec((1, D_A)),
            _const_spec((1, D_B)), _const_spec((1, D_B)),
            _const_spec((B_GROUPS, CHUNK, CHUNK)),
            _const_spec((CHUNK, B_GROUPS)),
        ],
        out_specs=(
            pl.BlockSpec((1, tm, D_MODEL), lambda b, t: (b, t, 0)),
            pl.BlockSpec((1, CONV_WIDTH - 1, D_A), lambda b, t: (b, 0, 0)),
            pl.BlockSpec((1, CHUNK, D_B), lambda b, t: (b, 0, 0)),
        ),
        scratch_shapes=[pltpu.VMEM((tm + CONV_HALO, D_A), F32)],
        compiler_params=_cparams("arbitrary", "arbitrary"),
        name="even_prompt",
    )(x, mods, g_mix, w_in, conv_w, conv_b, cln_g, cln_b, sln_g, sln_b, sgu_w, sgu_bt)


def _even_sample_body(x_ref, mod_ref, gmix_ref, win_ref, st_ref, cw_ref, cb_ref, clg_ref, clb_ref, slg_ref,
                      slb_ref, w00_ref, b0_ref, y_ref, a_ref, v_ref):
    h = _modulate(x_ref[...], gmix_ref[...], mod_ref[0], mod_ref[1]).astype(BF16)
    z = _dot(h, win_ref[...])
    a = z[:, :D_A] * _sigmoid(z[:, D_A:2 * D_A])
    u = _gelu(z[:, 2 * D_A:2 * D_A + D_B])
    v = _layernorm(_gelu(z[:, 2 * D_A + D_B:]), slg_ref[...], slb_ref[...])
    acc = a * cw_ref[CONV_WIDTH - 1:CONV_WIDTH, :] + cb_ref[...]
    for w in range(CONV_WIDTH - 1):
        acc = acc + st_ref[w] * cw_ref[w:w + 1, :]
    a_out = _silu(_layernorm(acc, clg_ref[...], clb_ref[...]))
    b_out = u * (w00_ref[...] * v + b0_ref[...])
    y_ref[:, :D_A] = a_out.astype(BF16)
    y_ref[:, D_A:] = b_out.astype(BF16)
    a_ref[...] = a
    v_ref[...] = v


def _even_sample(x, mods, g_mix, w_in, state_t, conv_w, conv_b, cln_g, cln_b, sln_g, sln_b, w00, b0):
    sn = x.shape[0]
    shapes = [a.shape for a in (x, mods, g_mix, w_in, state_t, conv_w, conv_b, cln_g, cln_b, sln_g, sln_b, w00, b0)]
    return pl.pallas_call(
        _even_sample_body,
        out_shape=(
            jax.ShapeDtypeStruct((sn, D_MODEL), BF16),
            jax.ShapeDtypeStruct((sn, D_A), F32),
            jax.ShapeDtypeStruct((sn, D_B), F32),
        ),
        grid=(1,),
        in_specs=[_const_spec(s) for s in shapes],
        out_specs=(_const_spec((sn, D_MODEL)), _const_spec((sn, D_A)), _const_spec((sn, D_B))),
        compiler_params=_cparams("arbitrary"),
        name="even_sample",
    )(x, mods, g_mix, w_in, state_t, conv_w, conv_b, cln_g, cln_b, sln_g, sln_b, w00, b0)


def _proj_body(x_ref, mod_ref, gmix_ref, w_ref, cos_ref, sin_ref,
               q_ref, qr_ref, ck_ref, cv_ref, sk_ref, sv_ref, wk_ref, wv_ref,
               skb_ref, svb_ref, wkb_ref, wvb_ref, g_ref):
    h = _modulate(x_ref[0], gmix_ref[...], mod_ref[0, 0], mod_ref[1, 0]).astype(BF16)
    z = _dot(h, w_ref[...])
    cos = cos_ref[...]
    sin = sin_ref[...]
    q0 = N_HEADS * HEAD_DIM
    q_ref[0] = z[:, :q0]
    for c in range(q0 // LANES):
        qr_ref[0, :, c * LANES:(c + 1) * LANES] = _rope_pair(z[:, c * LANES:(c + 1) * LANES], cos, sin)

    def kv(i):
        return z[:, q0 + i * KV_DIM:q0 + (i + 1) * KV_DIM]

    def kv_rot(i):
        base = q0 + i * KV_DIM
        return jnp.concatenate(
            [_rope_pair(z[:, base + c * LANES:base + (c + 1) * LANES], cos, sin) for c in range(N_PAIR)], axis=1)

    ck_ref[0] = kv(0)
    cv_ref[0] = kv(1)
    sk = kv_rot(2)
    sv = kv(3)
    wk = kv_rot(4)
    wv = kv(5)
    sk_ref[0] = sk
    sv_ref[0] = sv
    wk_ref[0] = wk
    wv_ref[0] = wv
    skb_ref[0] = sk.astype(BF16)
    svb_ref[0] = sv.astype(BF16)
    wkb_ref[0] = wk.astype(BF16)
    wvb_ref[0] = wv.astype(BF16)
    g_ref[0] = _sigmoid(z[:, q0 + 6 * KV_DIM:])


def _nsa_project(x, mods, g_mix, w_in, cos, sin, *, tm):
    bn, ln, _ = x.shape
    r = mods.shape[2]
    tm = min(tm, ln)
    mod_rows = 1 if r == 1 else tm
    mod_map = (lambda b, t: (0, b, 0, 0)) if r == 1 else (lambda b, t: (0, b, t, 0))
    tok = lambda w: pl.BlockSpec((1, tm, w), lambda b, t: (b, t, 0))
    sds = lambda w, dt: jax.ShapeDtypeStruct((bn, ln, w), dt)
    return pl.pallas_call(
        _proj_body,
        out_shape=(sds(D_MODEL, F32), sds(D_MODEL, F32)) + (sds(KV_DIM, F32),) * 6 + (sds(KV_DIM, BF16),) * 4
        + (sds(LANES, F32),),
        grid=(bn, ln // tm),
        in_specs=[
            tok(D_MODEL),
            pl.BlockSpec((6, 1, mod_rows, D_MODEL), mod_map),
            _const_spec((1, D_MODEL)),
            _const_spec((D_MODEL, ODD_IN_PAD)),
            pl.BlockSpec((tm, LANES), lambda b, t: (t, 0)),
            pl.BlockSpec((tm, LANES), lambda b, t: (t, 0)),
        ],
        out_specs=(tok(D_MODEL), tok(D_MODEL)) + (tok(KV_DIM),) * 10 + (tok(LANES),),
        compiler_params=_cparams("arbitrary", "arbitrary"),
        name="nsa_project",
    )(x, mods, g_mix, w_in, cos, sin)


def _compress_rows(load_rows, pe2_ref, w1_ref, w2e_ref, w2o_ref):
    n_chunk = 128
    lane = _lane_iota((n_chunk, LANES))
    low = lane < HEAD_DIM
    heads = [[] for _ in range(N_KV)]
    for p in range(CMP_STRIDE // 2):
        for m in range(N_PAIR):
            lo = load_rows(2 * p, m)
            hi = load_rows(2 * p + 1, m)
            heads[2 * m].append(jnp.where(low, lo, pltpu.roll(hi, HEAD_DIM, 1)))
            heads[2 * m + 1].append(jnp.where(low, pltpu.roll(lo, HEAD_DIM, 1), hi))
    x_all = jnp.concatenate([jnp.concatenate(hp, axis=1) for hp in heads], axis=0).astype(BF16)
    ab = _dot(x_all, w1_ref[...])
    pe = _dot(pe2_ref[...], w1_ref[...])
    hid0 = pe[0:1, :CMP_HIDDEN] + pe[1:2, CMP_HIDDEN:]
    hid = []
    for k in range(N_KV):
        a_part = ab[k * n_chunk:(k + 1) * n_chunk, :CMP_HIDDEN]
        b_part = ab[k * n_chunk:(k + 1) * n_chunk, CMP_HIDDEN:]
        nxt = pltpu.roll(b_part, n_chunk - 1, 0)
        hid.append(_gelu(a_part + nxt + hid0).astype(BF16))
    out = []
    for m in range(N_PAIR):
        out.append(_dot(hid[2 * m], w2e_ref[...]) + _dot(hid[2 * m + 1], w2o_ref[...]))
    return jnp.concatenate(out, axis=1)


def _compress_prompt_body(ck_ref, cv_ref, pek_ref, w1k_ref, w2ke_ref, w2ko_ref, pev_ref, w1v_ref, w2ve_ref, w2vo_ref,
                          kc_ref, vc_ref):
    n_chunk = ck_ref.shape[1] // (N_PAIR * CMP_STRIDE)
    assert n_chunk == 128

    def rows(ref):
        return lambda j, m: ref[0, pl.ds(N_PAIR * j + m, n_chunk, stride=N_PAIR * CMP_STRIDE), :]

    kc_ref[0] = _compress_rows(rows(ck_ref), pek_ref, w1k_ref, w2ke_ref, w2ko_ref).astype(BF16)
    vc_ref[0] = _compress_rows(rows(cv_ref), pev_ref, w1v_ref, w2ve_ref, w2vo_ref).astype(BF16)


def _compress_prompt(ck, cv, cw_k, cw_v):
    bn, ln, _ = ck.shape
    ck = ck.reshape(bn, N_PAIR * ln, LANES)
    cv = cv.reshape(bn, N_PAIR * ln, LANES)
    seq = pl.BlockSpec((1, N_PAIR * ln, LANES), lambda b: (b, 0, 0))
    wspecs = [_const_spec(w.shape) for w in cw_k + cw_v]
    out = pl.BlockSpec((1, 128, KV_DIM), lambda b: (b, 0, 0))
    return pl.pallas_call(
        _compress_prompt_body,
        out_shape=(jax.ShapeDtypeStruct((bn, 128, KV_DIM), BF16),) * 2,
        grid=(bn,),
        in_specs=[seq, seq] + wspecs,
        out_specs=(out, out),
        compiler_params=_cparams("arbitrary"),
        name="compress_prompt",
    )(ck, cv, *cw_k, *cw_v)


def _attn_prompt_body(q_ref, qr_ref, g_ref, kc_ref, vc_ref, sk_ref, sv_ref, wk_ref, wv_ref, ov_ref, o_ref, *, tq, tk):
    qi = pl.program_id(1)
    q0 = qi * tq
    pos = q0 + lax.broadcasted_iota(jnp.int32, (tq, 1), 0)
    pos4 = jnp.concatenate([pos] * GROUP, axis=0)
    lane = _lane_iota((1, LANES))
    gates = g_ref[0]

    cmask = (lane < LANES - 1) & (lane * CMP_STRIDE + (CMP_BLOCK - 1) <= pos4)
    o_cmp = []
    imp = jnp.zeros((tq, LANES), F32)
    for k in range(N_KV):
        m = k // 2
        qs = _stack_group_heads(q_ref, k, tq)
        s = _dot_nt(qs, kc_ref[0, :, m * LANES:(m + 1) * LANES]) * SCALE
        p = _masked_softmax(s, cmask)
        o_cmp.append(_dot(p.astype(BF16), vc_ref[0, :, m * LANES:(m + 1) * LANES]))
        psum = p[0:tq] + p[tq:2 * tq] + p[2 * tq:3 * tq] + p[3 * tq:4 * tq]
        imp = imp + jnp.dot(psum, ov_ref[k], preferred_element_type=F32, precision=HIGHEST)
    sel = _select_blocks(imp, pos, N_SEL).astype(BF16)

    def flash(qs, k_ref, v_ref, m, lo, hi, mask_fn):
        def body(kt, carry):
            m_i, l_i, acc = carry
            k0 = pl.multiple_of(kt * tk, tk)
            s = _dot_nt(qs, k_ref[0, pl.ds(k0, tk), m * LANES:(m + 1) * LANES]) * SCALE
            kpos = k0 + _lane_iota((tq, tk))
            ok = mask_fn(kt, kpos)
            ok4 = jnp.concatenate([ok] * GROUP, axis=0)
            s = jnp.where(ok4, s, NEG)
            m_new = jnp.maximum(m_i, jnp.max(s, -1, keepdims=True))
            alpha = jnp.exp(m_i - m_new)
            p = jnp.where(ok4, jnp.exp(s - m_new), 0.0)
            l_new = alpha * l_i + jnp.sum(p, -1, keepdims=True)
            acc = alpha * acc + _dot(p.astype(BF16), v_ref[0, pl.ds(k0, tk), m * LANES:(m + 1) * LANES])
            return m_new, l_new, acc

        init = (jnp.full((GROUP * tq, 1), NEG, F32), jnp.zeros((GROUP * tq, 1), F32),
                jnp.zeros((GROUP * tq, LANES), F32))
        _, l_f, acc = lax.fori_loop(lo, hi, body, init)
        return acc / jnp.maximum(l_f, 1e-30)

    n_kt = (q0 + tq + tk - 1) // tk
    win_lo = jnp.maximum(q0 - WINDOW, 0) // tk
    for k in range(N_KV):
        m = k // 2
        par = k % 2
        qrs = _stack_group_heads(qr_ref, k, tq)

        def sel_mask(kt, kpos, k=k):
            hit = _dot(sel, _block_expand(k, kt * tk, tk))
            return (hit > 0.5) & (kpos <= pos)

        def win_mask(kt, kpos):
            d = pos - kpos
            return (d >= 0) & (d <= WINDOW)

        o_sel = flash(qrs, sk_ref, sv_ref, m, 0, n_kt, sel_mask)
        o_win = flash(qrs, wk_ref, wv_ref, m, win_lo, n_kt, win_mask)
        blocks = []
        for g in range(GROUP):
            h = GROUP * k + g
            rows = slice(g * tq, (g + 1) * tq)
            o = (gates[:, 3 * h:3 * h + 1] * o_cmp[k][rows] + gates[:, 3 * h + 1:3 * h + 2] * o_sel[rows]
                 + gates[:, 3 * h + 2:3 * h + 3] * o_win[rows])
            if g % 2 != par:
                o = pltpu.roll(o, HEAD_DIM, 1)
            blocks.append(o)
        low = _lane_iota((tq, LANES)) < HEAD_DIM
        for c in range(GROUP // 2):
            col = (GROUP * k) // 2 + c
            o_ref[0, :, col * LANES:(col + 1) * LANES] = jnp.where(low, blocks[2 * c], blocks[2 * c + 1]).astype(BF16)


def _attn_prompt(q, qr, gates, kc, vc, skb, svb, wkb, wvb, ov, *, tq, tk):
    bn, ln, _ = q.shape
    assert ln % tq == 0 and ln % tk == 0 and ln <= SEL_LANES * SEL_BLOCK
    tok = lambda w: pl.BlockSpec((1, tq, w), lambda b, t: (b, t, 0))
    seq = lambda n: pl.BlockSpec((1, n, KV_DIM), lambda b, t: (b, 0, 0))
    return pl.pallas_call(
        functools.partial(_attn_prompt_body, tq=tq, tk=tk),
        out_shape=jax.ShapeDtypeStruct((bn, ln, D_MODEL), BF16),
        grid=(bn, ln // tq),
        in_specs=[tok(D_MODEL), tok(D_MODEL), tok(LANES), seq(128), seq(128), seq(ln), seq(ln), seq(ln), seq(ln),
                  _const_spec(ov.shape)],
        out_specs=tok(D_MODEL),
        compiler_params=_cparams("arbitrary", "arbitrary"),
        name="attn_prompt",
    )(q, qr, gates, kc, vc, skb, svb, wkb, wvb, ov)


def _attn_sample_body(pt_ref, q_ref, qr_ref, g_ref, skn_ref, svn_ref, wkn_ref, wvn_ref, wks_ref, wvs_ref, *rest,
                      n_pages):
    del pt_ref
    pools = [rest[i * n_pages:(i + 1) * n_pages] for i in range(4)]
    (pek_ref, w1k_ref, w2ke_ref, w2ko_ref, pev_ref, w1v_ref, w2ve_ref, w2vo_ref, ov_ref, o_ref) = rest[4 * n_pages:]
    past = n_pages * PAGE_SIZE
    rows_per_page = PAGE_SIZE // CMP_STRIDE

    def strided(pages):
        def load(j, m):
            return jnp.concatenate(
                [pg[0, 0, pl.ds(N_PAIR * j + m, rows_per_page, stride=N_PAIR * CMP_STRIDE), :] for pg in pages],
                axis=0)
        return load

    kc = _compress_rows(strided(pools[0]), pek_ref, w1k_ref, w2ke_ref, w2ko_ref).astype(BF16)
    vc = _compress_rows(strided(pools[1]), pev_ref, w1v_ref, w2ve_ref, w2vo_ref).astype(BF16)

    lane = _lane_iota((SUBLANES, LANES))
    row = lax.broadcasted_iota(jnp.int32, (SUBLANES, LANES), 0)
    low_half = lane < HEAD_DIM
    upper_rows = row >= GROUP

    def pair_heads(ref, m):
        parts = []
        for kk in range(2):
            for g in range(GROUP):
                h = GROUP * (2 * m + kk) + g
                v = ref[0, :, (h // 2) * LANES:(h // 2 + 1) * LANES]
                if h % 2 != kk:
                    v = pltpu.roll(jnp.broadcast_to(v, (SUBLANES, LANES)), HEAD_DIM, 1)[0:1]
                parts.append(v)
        x = jnp.concatenate(parts, axis=0)
        keep = jnp.logical_xor(upper_rows, low_half)
        return jnp.where(keep, x, 0.0).astype(BF16)

    gates = g_ref[0]
    pos = jnp.full((SUBLANES, 1), past, jnp.int32)

    cmask = lane < LANES - 1
    p_cmp, o_cmp = [], []
    imp = jnp.zeros((SUBLANES, LANES), F32)
    for m in range(N_PAIR):
        qs = pair_heads(q_ref, m)
        s = _dot_nt(qs, kc[:, m * LANES:(m + 1) * LANES]) * SCALE
        p = _masked_softmax(s, cmask)
        o_cmp.append(_dot(p.astype(BF16), vc[:, m * LANES:(m + 1) * LANES]))
        for kk in range(2):
            psum = jnp.sum(p[kk * GROUP:(kk + 1) * GROUP], axis=0, keepdims=True)
            imp = imp + jnp.dot(jnp.broadcast_to(psum, (SUBLANES, LANES)), ov_ref[2 * m + kk],
                                preferred_element_type=F32, precision=HIGHEST)
    sel = _select_blocks(imp, pos, N_SEL - 1)
    seg = lane >> SEL_LANES_SHIFT
    expand = _block_expand(None, 0, past)

    for m in range(N_PAIR):
        qrs = pair_heads(qr_ref, m)
        want = jnp.where(upper_rows, 2 * m + 1, 2 * m)
        sel_rows = jnp.where(seg == want, sel, 0.0).astype(BF16)
        ok = _dot(sel_rows, expand) > 0.5
        s_parts = [_dot_nt(qrs, pg[0, 0, :, m * LANES:(m + 1) * LANES].astype(BF16)) for pg in pools[2]]
        s = jnp.concatenate(s_parts, axis=1) * SCALE
        s_new = jnp.sum(qrs.astype(F32) * skn_ref[0, :, m * LANES:(m + 1) * LANES].astype(BF16).astype(F32),
                        -1, keepdims=True) * SCALE
        s = jnp.where(ok, s, NEG)
        mx = jnp.maximum(jnp.max(s, -1, keepdims=True), s_new)
        p = jnp.where(ok, jnp.exp(s - mx), 0.0)
        p_new = jnp.exp(s_new - mx)
        denom = jnp.maximum(jnp.sum(p, -1, keepdims=True) + p_new, 1e-30)
        pb = p.astype(BF16)
        acc = p_new.astype(BF16).astype(F32) * svn_ref[0, :, m * LANES:(m + 1) * LANES].astype(BF16).astype(F32)
        for i, pg in enumerate(pools[3]):
            acc = acc + _dot(pb[:, i * PAGE_SIZE:(i + 1) * PAGE_SIZE],
                             pg[0, 0, :, m * LANES:(m + 1) * LANES].astype(BF16))
        o_sel = acc / denom
        sw = _dot_nt(qrs, wks_ref[0, :, m * LANES:(m + 1) * LANES].astype(BF16)) * SCALE
        sw_new = jnp.sum(qrs.astype(F32) * wkn_ref[0, :, m * LANES:(m + 1) * LANES].astype(BF16).astype(F32),
                         -1, keepdims=True) * SCALE
        mw = jnp.maximum(jnp.max(sw, -1, keepdims=True), sw_new)
        pw = jnp.exp(sw - mw)
        pw_new = jnp.exp(sw_new - mw)
        dw = jnp.maximum(jnp.sum(pw, -1, keepdims=True) + pw_new, 1e-30)
        accw = (_dot(pw.astype(BF16), wvs_ref[0, :, m * LANES:(m + 1) * LANES].astype(BF16))
                + pw_new.astype(BF16).astype(F32) * wvn_ref[0, :, m * LANES:(m + 1) * LANES].astype(BF16).astype(F32))
        o_win = accw / dw
        for kk in range(2):
            for c in range(GROUP // 2):
                halves = []
                for gg in range(2):
                    g = 2 * c + gg
                    h = GROUP * (2 * m + kk) + g
                    r = kk * GROUP + g
                    o = (gates[:, 3 * h:3 * h + 1] * o_cmp[m][r:r + 1] + gates[:, 3 * h + 1:3 * h + 2] * o_sel[r:r + 1]
                         + gates[:, 3 * h + 2:3 * h + 3] * o_win[r:r + 1])
                    if gg != kk:
                        o = pltpu.roll(jnp.broadcast_to(o, (SUBLANES, LANES)), HEAD_DIM, 1)[0:1]
                    halves.append(o)
                col = (GROUP * (2 * m + kk)) // 2 + c
                o_ref[0, :, col * LANES:(col + 1) * LANES] = jnp.where(
                    _lane_iota((1, LANES)) < HEAD_DIM, halves[0], halves[1]).astype(BF16)


def _attn_sample(page_table, layer, q, qr, gates, sk_new, sv_new, wk_new, wv_new, win_k, win_v,
                 pool_ck, pool_cv, pool_sk, pool_sv, cw_k, cw_v, ov):
    sn = q.shape[0]
    n_pages = page_table.shape[1]
    nbuf = win_k.shape[1]
    assert nbuf == WINDOW and n_pages * PAGE_SIZE == SEL_LANES * SEL_BLOCK
    one = lambda w: pl.BlockSpec((1, 1, w), lambda s, pt: (s, 0, 0))
    page_map = lambda s, pt, p: (layer, pt[s, p], 0, 0)
    page_specs = [pl.BlockSpec((1, 1, PAGE_SIZE, KV_DIM), functools.partial(page_map, p=p)) for p in range(n_pages)]
    pair_specs = [pl.BlockSpec((1, 1, N_PAIR * PAGE_SIZE, LANES), functools.partial(page_map, p=p))
                  for p in range(n_pages)]
    as_pairs = lambda pool: pool.reshape(pool.shape[0], pool.shape[1], N_PAIR * PAGE_SIZE, LANES)
    pool_ck, pool_cv = as_pairs(pool_ck), as_pairs(pool_cv)
    weights = cw_k + cw_v + (ov,)
    grid_spec = pltpu.PrefetchScalarGridSpec(
        num_scalar_prefetch=1,
        grid=(sn,),
        in_specs=[one(D_MODEL), one(D_MODEL), one(LANES), one(KV_DIM), one(KV_DIM), one(KV_DIM), one(KV_DIM),
                  pl.BlockSpec((1, nbuf, KV_DIM), lambda s, pt: (s, 0, 0)),
                  pl.BlockSpec((1, nbuf, KV_DIM), lambda s, pt: (s, 0, 0))]
        + pair_specs * 2 + page_specs * 2
        + [pl.BlockSpec(w.shape, functools.partial(lambda s, pt, n: (0,) * n, n=w.ndim)) for w in weights],
        out_specs=one(D_MODEL),
    )
    pools = [pool_ck] * n_pages + [pool_cv] * n_pages + [pool_sk] * n_pages + [pool_sv] * n_pages
    return pl.pallas_call(
        functools.partial(_attn_sample_body, n_pages=n_pages),
        out_shape=jax.ShapeDtypeStruct((sn, 1, D_MODEL), BF16),
        grid_spec=grid_spec,
        compiler_params=_cparams("arbitrary"),
        name="attn_sample",
    )(page_table, q, qr, gates, sk_new, sv_new, wk_new, wv_new, win_k, win_v, *pools, *weights)


def _rope_tables(pos):
    half = HEAD_DIM // 2
    inv = jnp.power(jnp.float32(ROPE_THETA), -jnp.arange(half, dtype=F32) * (2.0 / HEAD_DIM))
    ang = pos.astype(F32)[:, None] * inv[None, :]
    cos, sin = jnp.cos(ang), jnp.sin(ang)
    return jnp.concatenate([cos, cos, cos, cos], -1), jnp.concatenate([-sin, sin, -sin, sin], -1)


def _overlap_tables():
    n_cmp, n_sel = LANES - 1, SEL_LANES
    i = np.arange(n_cmp)[:, None] * CMP_STRIDE
    j = np.arange(n_sel)[None, :] * SEL_BLOCK
    ov = ((i < j + SEL_BLOCK) & (i + CMP_BLOCK > j)).astype(np.float32)
    out = np.zeros((N_KV, LANES, LANES), np.float32)
    for k in range(N_KV):
        out[k, :n_cmp, k * SEL_LANES:(k + 1) * SEL_LANES] = ov
    return jnp.asarray(out)


def _compress_weights(pe, w1, w2):
    r = CMP_BLOCK // CMP_STRIDE
    flat = CMP_STRIDE * HEAD_DIM
    w1cat = jnp.concatenate([w1[m * CMP_STRIDE:(m + 1) * CMP_STRIDE].reshape(flat, CMP_HIDDEN) for m in range(r)], 1)
    pe2 = jnp.zeros((SUBLANES, flat), F32).at[:r].set(pe.reshape(r, flat))
    zeros = jnp.zeros_like(w2)
    return (pe2.astype(BF16), w1cat.astype(BF16), jnp.concatenate([w2, zeros], 1).astype(BF16),
            jnp.concatenate([zeros, w2], 1).astype(BF16))


def kernel(x_prompt, x_sample, state_conv, cache_cmp_k, cache_cmp_v, cache_sel_k, cache_sel_v, state_win_k, state_win_v, page_table, c_prompt, c_sample, ada_w, ada_b, norm_mix_g, norm_ffn_g, ffn_w1, ffn_w2, even_w_in, even_w_out, conv_w, conv_b, conv_ln_g, conv_ln_b, sgu_ln_g, sgu_ln_b, sgu_w, sgu_b, odd_w_in, odd_w_out, cmp_pe_k, cmp_w1_k, cmp_w2_k, cmp_pe_v, cmp_w1_v, cmp_w2_v, final_norm_g):
    depth = ada_w.shape[0]
    bn, ln, _ = x_prompt.shape
    sn = x_sample.shape[0]
    n_pool = cache_cmp_k.shape[1]
    past = page_table.shape[1] * PAGE_SIZE

    ada = _ada_all(jnp.concatenate([c_prompt, c_sample], 0), ada_w, ada_b)
    ov = _overlap_tables()
    cos_p, sin_p = _rope_tables(jnp.arange(ln))
    cos_s, sin_s = _rope_tables(jnp.full((sn,), past))
    g_fin = final_norm_g.reshape(1, D_MODEL)
    row = lambda a: a.reshape(1, -1)
    pools = [c.reshape(c.shape[0], n_pool, PAGE_SIZE, KV_DIM) for c in (cache_cmp_k, cache_cmp_v, cache_sel_k, cache_sel_v)]

    xp = x_prompt
    xs = x_sample.reshape(1, sn, D_MODEL)
    conv_p, conv_s, chv_p, chv_s = [], [], [], []
    nsa_p = [[] for _ in range(6)]
    nsa_s = [[] for _ in range(6)]
    for l in range(depth):
        mods_p = ada[l, :, :bn].reshape(6, bn, 1, D_MODEL)
        mods_s = ada[l, :, bn:].reshape(6, 1, sn, D_MODEL)
        g_mix = row(norm_mix_g[l])
        if l % 2 == 0:
            e = l // 2
            w_in = even_w_in[e].astype(BF16)
            w_out = even_w_out[e].astype(BF16)
            cln = (row(conv_b[e]), row(conv_ln_g[e]), row(conv_ln_b[e]), row(sgu_ln_g[e]), row(sgu_ln_b[e]))
            yp, bp, vp = _even_prompt(xp, mods_p, g_mix, w_in, conv_w[e], *cln, sgu_w[e], sgu_b[e].T, tm=256)
            w00 = row(jnp.repeat(sgu_w[e, :, 0, 0], B_GROUP_DIM))
            b0 = row(jnp.repeat(sgu_b[e, :, 0], B_GROUP_DIM))
            ys, a_s, v_s = _even_sample(xs[0], mods_s[:, 0], g_mix, w_in, jnp.swapaxes(state_conv[e], 0, 1),
                                        conv_w[e], *cln, w00, b0)
            conv_p.append(bp)
            conv_s.append(jnp.concatenate([state_conv[e][:, 1:], a_s[:, None, :]], axis=1))
            chv_p.append(vp)
            chv_s.append(v_s[:, None, :])
            ys = ys[None]
        else:
            o = l // 2
            w_in = jnp.pad(odd_w_in[o], ((0, 0), (0, ODD_IN_PAD - odd_w_in.shape[-1]))).astype(BF16)
            w_out = odd_w_out[o].astype(BF16)
            cw_k = _compress_weights(cmp_pe_k[o], cmp_w1_k[o], cmp_w2_k[o])
            cw_v = _compress_weights(cmp_pe_v[o], cmp_w1_v[o], cmp_w2_v[o])
            (q, qr, ck, cv, sk, sv, wk, wv, skb, svb, wkb, wvb, gates) = _nsa_project(
                xp, mods_p, g_mix, w_in, cos_p, sin_p, tm=256)
            kc, vc = _compress_prompt(ck, cv, cw_k, cw_v)
            yp = _attn_prompt(q, qr, gates, kc, vc, skb, svb, wkb, wvb, ov, tq=256, tk=256)
            nw = min(WINDOW, ln)
            for i, t in enumerate((ck, cv, sk, sv, wk[:, ln - nw:], wv[:, ln - nw:])):
                nsa_p[i].append(t.reshape(bn, -1, N_KV, HEAD_DIM))
            (q, qr, ck, cv, sk, sv, wk, wv, _, _, _, _, gates) = _nsa_project(
                xs, mods_s, g_mix, w_in, cos_s, sin_s, tm=sn)
            tok = lambda t: t.reshape(sn, 1, t.shape[-1])
            win_k = state_win_k[o].reshape(sn, -1, KV_DIM)
            win_v = state_win_v[o].reshape(sn, -1, KV_DIM)
            ys = _attn_sample(page_table, o, tok(q), tok(qr), tok(gates), tok(sk), tok(sv), tok(wk), tok(wv),
                              win_k, win_v, *pools, cw_k, cw_v, ov).reshape(1, sn, D_MODEL)
            nwin = min(WINDOW, past + 1)
            kw = jnp.concatenate([win_k, tok(wk)], axis=1)[:, -nwin:]
            vw = jnp.concatenate([win_v, tok(wv)], axis=1)[:, -nwin:]
            for i, t in enumerate((tok(ck), tok(cv), tok(sk), tok(sv), kw, vw)):
                nsa_s[i].append(t.reshape(sn, -1, N_KV, HEAD_DIM))
        final = l == depth - 1
        w1 = ffn_w1[l].astype(BF16)
        w2 = ffn_w2[l].astype(BF16)
        g_ffn = row(norm_ffn_g[l])
        xp = _post(xp, yp, mods_p, g_ffn, w_out, w1, w2, g_fin, final=final, tm=512)
        xs = _post(xs, ys, mods_s, g_ffn, w_out, w1, w2, g_fin, final=final, tm=sn)
    st = lambda lst: jnp.stack(lst, axis=0)
    return (xp, xs.reshape(sn, 1, D_MODEL),
            st(conv_p), st(conv_s), st(chv_p), st(chv_s),
            st(nsa_p[0]), st(nsa_p[1]), st(nsa_p[2]), st(nsa_p[3]), st(nsa_p[4]), st(nsa_p[5]),
            st(nsa_s[0]), st(nsa_s[1]), st(nsa_s[2]), st(nsa_s[3]), st(nsa_s[4]), st(nsa_s[5]))
```

```python
import functools

import numpy as np
import jax
import jax.numpy as jnp
from jax import lax
from jax.experimental import pallas as pl
from jax.experimental.pallas import tpu as pltpu

F32 = jnp.float32
BF16 = jnp.bfloat16
HIGHEST = lax.Precision.HIGHEST

LANES = 128
SUBLANES = 8
VMEM_LIMIT_BYTES = 56 * 1024 * 1024

D_MODEL = 1024
D_FF = 4 * D_MODEL
D_A = D_MODEL // 2
CONV_WIDTH = 31
CONV_HALO = 32
D_B = D_MODEL // 2
B_GROUPS = 4
B_GROUP_DIM = D_B // B_GROUPS
CHUNK = 128
N_HEADS = 16
HEAD_DIM = 64
N_KV = 4
GROUP = N_HEADS // N_KV
KV_DIM = N_KV * HEAD_DIM
N_PAIR = KV_DIM // LANES
CMP_BLOCK = 32
CMP_STRIDE = 16
CMP_HIDDEN = 2 * HEAD_DIM
SEL_BLOCK = 64
N_SEL = 8
SEL_LANES = 32
SEL_SHIFT = 6
SEL_LANES_SHIFT = 5
WINDOW = 512
PAGE_SIZE = 128
ROPE_THETA = 10000.0
EPS = 1e-6
NEG = -1e30
MASK_BIAS = -(2.0 ** 100)
FORCED = 1e4
SCALE = HEAD_DIM ** -0.5
ODD_IN_PAD = N_HEADS * HEAD_DIM + 6 * KV_DIM + LANES


def _cparams(*sem):
    return pltpu.CompilerParams(dimension_semantics=sem, vmem_limit_bytes=VMEM_LIMIT_BYTES)


def _const_spec(shape):
    n = len(shape)
    return pl.BlockSpec(shape, lambda *_: (0,) * n)


def _sigmoid(x):
    return 1.0 / (1.0 + jnp.exp(-x))


def _silu(x):
    return x * _sigmoid(x)


def _gelu(x):
    return 0.5 * x * (1.0 + jnp.tanh(np.sqrt(2.0 / np.pi).astype(np.float32) * (x + 0.044715 * (x * x * x))))


def _rmsnorm(x, g):
    return x * lax.rsqrt(jnp.mean(x * x, -1, keepdims=True) + EPS) * g


def _layernorm(x, g, b):
    xc = x - jnp.mean(x, -1, keepdims=True)
    return xc * lax.rsqrt(jnp.mean(xc * xc, -1, keepdims=True) + EPS) * g + b


def _modulate(x, g, shift, scale):
    return _rmsnorm(x, g) * (1.0 + scale) + shift


def _dot(a, b):
    return jnp.dot(a, b, preferred_element_type=F32)


def _dot_nt(a, b):
    return lax.dot_general(a, b, (((1,), (1,)), ((), ())), preferred_element_type=F32)


def _lane_iota(shape):
    return lax.broadcasted_iota(jnp.int32, shape, len(shape) - 1)


def _masked_softmax(s, mask):
    s = jnp.where(mask, s, NEG)
    m = jnp.max(s, -1, keepdims=True)
    p = jnp.where(mask, jnp.exp(s - m), 0.0)
    return p / jnp.maximum(jnp.sum(p, -1, keepdims=True), 1e-30)


def _rope_pair(xs, cos, sin_signed):
    half = HEAD_DIM // 2
    first = (_lane_iota(xs.shape) & (HEAD_DIM - 1)) < half
    swapped = jnp.where(first, pltpu.roll(xs, LANES - half, 1), pltpu.roll(xs, half, 1))
    return xs * cos + swapped * sin_signed


def _select_blocks(imp, pos, n_top):
    lane = _lane_iota(imp.shape)
    j = lane & (SEL_LANES - 1)
    cur = pos >> SEL_SHIFT
    valid = j * SEL_BLOCK <= pos
    forced = (j == 0) | (j == cur) | (j == cur - 1)
    score = jnp.where(valid, jnp.where(forced, FORCED, imp), NEG)
    rank = jnp.zeros(imp.shape, F32)
    for d in range(1, SEL_LANES):
        nonwrap = j >= d
        rot = jnp.where(nonwrap, pltpu.roll(score, d, 1), pltpu.roll(score, (d - SEL_LANES) % LANES, 1))
        ahead = (rot > score) | ((rot == score) & nonwrap)
        rank = rank + jnp.where(ahead, 1.0, 0.0)
    return jnp.where(rank < n_top, 1.0, 0.0)


def _block_expand(n_keys):
    r = lax.broadcasted_iota(jnp.int32, (LANES, n_keys), 0)
    c = lax.broadcasted_iota(jnp.int32, (LANES, n_keys), 1)
    return jnp.where((r & (SEL_LANES - 1)) == (c >> SEL_SHIFT), 1.0, 0.0).astype(BF16)


def _ada_body(c_ref, w_ref, b_ref, o_ref):
    a = _silu(c_ref[...])
    o_ref[0, 0] = jnp.dot(a, w_ref[0], preferred_element_type=F32, precision=HIGHEST) + b_ref[0, 0]


def _ada_all(c_all, ada_w, ada_b):
    depth = ada_w.shape[0]
    rows = c_all.shape[0]
    return pl.pallas_call(
        _ada_body,
        out_shape=jax.ShapeDtypeStruct((depth, 6, rows, D_MODEL), F32),
        grid=(depth, 6),
        in_specs=[
            _const_spec((rows, D_MODEL)),
            pl.BlockSpec((1, D_MODEL, D_MODEL), lambda l, j: (l, 0, j)),
            pl.BlockSpec((1, 1, 1, D_MODEL), lambda l, j: (l, j, 0, 0)),
        ],
        out_specs=pl.BlockSpec((1, 1, rows, D_MODEL), lambda l, j: (l, j, 0, 0)),
        compiler_params=_cparams("arbitrary", "arbitrary"),
        name="ada_params",
    )(c_all, ada_w, ada_b.reshape(depth, 6, 1, D_MODEL))


def _post_body(x_ref, y_ref, mod_ref, gffn_ref, wo_ref, w1_ref, w2_ref, gfin_ref, o_ref, *, final, ff_chunk):
    x = x_ref[0]
    x = x + mod_ref[2, 0] * _dot(y_ref[0], wo_ref[...])
    h = _modulate(x, gffn_ref[...], mod_ref[3, 0], mod_ref[4, 0]).astype(BF16)
    acc = jnp.zeros(x.shape, F32)
    for c in range(D_FF // ff_chunk):
        t = _dot(h, w1_ref[:, c * ff_chunk:(c + 1) * ff_chunk])
        t = jnp.square(jnp.maximum(t, 0.0)).astype(BF16)
        acc = acc + _dot(t, w2_ref[c * ff_chunk:(c + 1) * ff_chunk, :])
    x = x + mod_ref[5, 0] * acc
    if final:
        x = _rmsnorm(x, gfin_ref[...])
    o_ref[0] = x


def _post(x, y, mods, g_ffn, w_out, w1, w2, g_fin, *, final, tm):
    bn, ln, _ = x.shape
    d_in = y.shape[-1]
    r = mods.shape[2]
    tm = min(tm, ln)
    mod_rows = 1 if r == 1 else tm
    mod_map = (lambda b, t: (0, b, 0, 0)) if r == 1 else (lambda b, t: (0, b, t, 0))
    return pl.pallas_call(
        functools.partial(_post_body, final=final, ff_chunk=1024),
        out_shape=jax.ShapeDtypeStruct(x.shape, F32),
        grid=(bn, ln // tm),
        in_specs=[
            pl.BlockSpec((1, tm, D_MODEL), lambda b, t: (b, t, 0)),
            pl.BlockSpec((1, tm, d_in), lambda b, t: (b, t, 0)),
            pl.BlockSpec((6, 1, mod_rows, D_MODEL), mod_map),
            _const_spec((1, D_MODEL)),
            _const_spec((d_in, D_MODEL)),
            _const_spec((D_MODEL, D_FF)),
            _const_spec((D_FF, D_MODEL)),
            _const_spec((1, D_MODEL)),
        ],
        out_specs=pl.BlockSpec((1, tm, D_MODEL), lambda b, t: (b, t, 0)),
        compiler_params=_cparams("arbitrary", "arbitrary"),
        name="post_mlp",
    )(x, y, mods, g_ffn, w_out, w1, w2, g_fin)


def _even_prompt_body(x_ref, mod_ref, gmix_ref, win_ref, cw_ref, cb_ref, clg_ref, clb_ref, slg_ref, slb_ref,
                      sw_ref, sbt_ref, y_ref, conv_ref, chv_ref, ext_ref, *, tm, conv_rows):
    t = pl.program_id(1)
    h = _modulate(x_ref[0], gmix_ref[...], mod_ref[0, 0], mod_ref[1, 0]).astype(BF16)
    z = _dot(h, win_ref[...])
    a = z[:, :D_A] * _sigmoid(z[:, D_A:2 * D_A])
    u = _gelu(z[:, 2 * D_A:2 * D_A + D_B])
    v = _layernorm(_gelu(z[:, 2 * D_A + D_B:]), slg_ref[...], slb_ref[...])

    @pl.when(t == 0)
    def _():
        ext_ref[0:CONV_HALO, :] = jnp.zeros((CONV_HALO, D_A), F32)

    ext_ref[CONV_HALO:CONV_HALO + tm, :] = a
    first = CONV_HALO - (CONV_WIDTH - 1)
    pieces = []
    for c in range(tm // conv_rows):
        acc = jnp.zeros((conv_rows, D_A), F32) + cb_ref[...]
        for w in range(CONV_WIDTH):
            r0 = first + w + c * conv_rows
            acc = acc + ext_ref[r0:r0 + conv_rows, :] * cw_ref[w:w + 1, :]
        pieces.append(acc)
    a_conv = jnp.concatenate(pieces, axis=0)
    conv_ref[0] = ext_ref[tm + first:tm + CONV_HALO, :]
    ext_ref[0:CONV_HALO, :] = ext_ref[tm:tm + CONV_HALO, :]
    a_out = _silu(_layernorm(a_conv, clg_ref[...], clb_ref[...]))

    row = lax.broadcasted_iota(jnp.int32, (CHUNK, CHUNK), 0)
    col = lax.broadcasted_iota(jnp.int32, (CHUNK, CHUNK), 1)
    wm = [jnp.where(col <= row, sw_ref[g], 0.0).astype(BF16) for g in range(B_GROUPS)]
    vb = v.astype(BF16)
    gate_rows = []
    for c in range(tm // CHUNK):
        cols = []
        for g in range(B_GROUPS):
            vg = vb[c * CHUNK:(c + 1) * CHUNK, g * B_GROUP_DIM:(g + 1) * B_GROUP_DIM]
            cols.append(_dot(wm[g], vg) + sbt_ref[:, g:g + 1])
        gate_rows.append(jnp.concatenate(cols, axis=1))
    b_out = u * jnp.concatenate(gate_rows, axis=0)

    y_ref[0, :, :D_A] = a_out.astype(BF16)
    y_ref[0, :, D_A:] = b_out.astype(BF16)
    chv_ref[0] = v[tm - CHUNK:, :]


def _even_prompt(x, mods, g_mix, w_in, conv_w, conv_b, cln_g, cln_b, sln_g, sln_b, sgu_w, sgu_bt, *, tm):
    bn, ln, _ = x.shape
    assert ln % tm == 0 and tm % CHUNK == 0 and ln >= CONV_HALO
    return pl.pallas_call(
        functools.partial(_even_prompt_body, tm=tm, conv_rows=64),
        out_shape=(
            jax.ShapeDtypeStruct((bn, ln, D_MODEL), BF16),
            jax.ShapeDtypeStruct((bn, CONV_WIDTH - 1, D_A), F32),
            jax.ShapeDtypeStruct((bn, CHUNK, D_B), F32),
        ),
        grid=(bn, ln // tm),
        in_specs=[
            pl.BlockSpec((1, tm, D_MODEL), lambda b, t: (b, t, 0)),
            pl.BlockSpec((6, 1, 1, D_MODEL), lambda b, t: (0, b, 0, 0)),
            _const_spec((1, D_MODEL)),
            _const_spec((D_MODEL, 2 * D_A + 2 * D_B)),
            _const_spec((CONV_WIDTH, D_A)),
            _const_spec((1, D_A)), _const_spec((1, D_A)), _const_spec((1, D_A)),
            _const_spec((1, D_B)), _const_spec((1, D_B)),
            _const_spec((B_GROUPS, CHUNK, CHUNK)),
            _const_spec((CHUNK, B_GROUPS)),
        ],
        out_specs=(
            pl.BlockSpec((1, tm, D_MODEL), lambda b, t: (b, t, 0)),
            pl.BlockSpec((1, CONV_WIDTH - 1, D_A), lambda b, t: (b, 0, 0)),
            pl.BlockSpec((1, CHUNK, D_B), lambda b, t: (b, 0, 0)),
        ),
        scratch_shapes=[pltpu.VMEM((tm + CONV_HALO, D_A), F32)],
        compiler_params=_cparams("arbitrary", "arbitrary"),
        name="even_prompt",
    )(x, mods, g_mix, w_in, conv_w, conv_b, cln_g, cln_b, sln_g, sln_b, sgu_w, sgu_bt)


def _even_sample_body(x_ref, mod_ref, gmix_ref, win_ref, st_ref, cw_ref, cb_ref, clg_ref, clb_ref, slg_ref,
                      slb_ref, w00_ref, b0_ref, y_ref, a_ref, v_ref):
    h = _modulate(x_ref[...], gmix_ref[...], mod_ref[0], mod_ref[1]).astype(BF16)
    z = _dot(h, win_ref[...])
    a = z[:, :D_A] * _sigmoid(z[:, D_A:2 * D_A])
    u = _gelu(z[:, 2 * D_A:2 * D_A + D_B])
    v = _layernorm(_gelu(z[:, 2 * D_A + D_B:]), slg_ref[...], slb_ref[...])
    acc = a * cw_ref[CONV_WIDTH - 1:CONV_WIDTH, :] + cb_ref[...]
    for w in range(CONV_WIDTH - 1):
        acc = acc + st_ref[w] * cw_ref[w:w + 1, :]
    a_out = _silu(_layernorm(acc, clg_ref[...], clb_ref[...]))
    b_out = u * (w00_ref[...] * v + b0_ref[...])
    y_ref[:, :D_A] = a_out.astype(BF16)
    y_ref[:, D_A:] = b_out.astype(BF16)
    a_ref[...] = a
    v_ref[...] = v


def _even_sample(x, mods, g_mix, w_in, state_t, conv_w, conv_b, cln_g, cln_b, sln_g, sln_b, w00, b0):
    sn = x.shape[0]
    shapes = [a.shape for a in (x, mods, g_mix, w_in, state_t, conv_w, conv_b, cln_g, cln_b, sln_g, sln_b, w00, b0)]
    return pl.pallas_call(
        _even_sample_body,
        out_shape=(
            jax.ShapeDtypeStruct((sn, D_MODEL), BF16),
            jax.ShapeDtypeStruct((sn, D_A), F32),
            jax.ShapeDtypeStruct((sn, D_B), F32),
        ),
        grid=(1,),
        in_specs=[_const_spec(s) for s in shapes],
        out_specs=(_const_spec((sn, D_MODEL)), _const_spec((sn, D_A)), _const_spec((sn, D_B))),
        compiler_params=_cparams("arbitrary"),
        name="even_sample",
    )(x, mods, g_mix, w_in, state_t, conv_w, conv_b, cln_g, cln_b, sln_g, sln_b, w00, b0)


def _proj_body(x_ref, mod_ref, gmix_ref, w_ref, cos_ref, sin_ref,
               qs_ref, qrs_ref, rowc_ref, rows_ref, cols_ref, colsb_ref, g_ref):
    tm = x_ref.shape[1]
    h = _modulate(x_ref[0], gmix_ref[...], mod_ref[0, 0], mod_ref[1, 0]).astype(BF16)
    z = _dot(h, w_ref[...])
    cos = cos_ref[...]
    sin = sin_ref[...]
    q0 = N_HEADS * HEAD_DIM
    low = _lane_iota((tm, LANES)) < HEAD_DIM
    for c in range(q0 // LANES):
        xs = z[:, c * LANES:(c + 1) * LANES] * SCALE
        xr = _rope_pair(xs, cos, sin)
        for half in range(2):
            hd = 2 * c + half
            par = (hd // GROUP) % 2
            keep = low if par == 0 else jnp.logical_not(low)
            for src, ref in ((xs, qs_ref), (xr, qrs_ref)):
                v = src if half == par else pltpu.roll(src, HEAD_DIM, 1)
                ref[0, hd] = jnp.where(keep, v, 0.0).astype(BF16)

    def kv(i):
        return z[:, q0 + i * KV_DIM:q0 + (i + 1) * KV_DIM]

    def kv_rot(i):
        base = q0 + i * KV_DIM
        return jnp.concatenate(
            [_rope_pair(z[:, base + c * LANES:base + (c + 1) * LANES], cos, sin) for c in range(N_PAIR)], axis=1)

    rows = jnp.concatenate([kv(0), kv(1), kv_rot(2), kv(3), kv_rot(4), kv(5)], axis=1)
    rowc_ref[0] = rows[:, :2 * KV_DIM]
    rows_ref[0] = rows[:, 2 * KV_DIM:]
    cols = rows.T
    cols_ref[0] = cols
    colsb_ref[0] = cols[2 * KV_DIM:].astype(BF16)
    g_ref[0] = _sigmoid(z[:, q0 + 6 * KV_DIM:])


def _nsa_project(x, mods, g_mix, w_in, cos, sin, *, tm):
    bn, ln, _ = x.shape
    r = mods.shape[2]
    tm = min(tm, ln)
    mod_rows = 1 if r == 1 else tm
    mod_map = (lambda b, t: (0, b, 0, 0)) if r == 1 else (lambda b, t: (0, b, t, 0))
    tok = lambda w: pl.BlockSpec((1, tm, w), lambda b, t: (b, t, 0))
    col = lambda w: pl.BlockSpec((1, w, tm), lambda b, t: (b, 0, t))
    head = pl.BlockSpec((1, N_HEADS, tm, LANES), lambda b, t: (b, 0, t, 0))
    return pl.pallas_call(
        _proj_body,
        out_shape=(
            jax.ShapeDtypeStruct((bn, N_HEADS, ln, LANES), BF16),
            jax.ShapeDtypeStruct((bn, N_HEADS, ln, LANES), BF16),
            jax.ShapeDtypeStruct((bn, ln, 2 * KV_DIM), F32),
            jax.ShapeDtypeStruct((bn, ln, 4 * KV_DIM), F32),
            jax.ShapeDtypeStruct((bn, 6 * KV_DIM, ln), F32),
            jax.ShapeDtypeStruct((bn, 4 * KV_DIM, ln), BF16),
            jax.ShapeDtypeStruct((bn, ln, LANES), F32),
        ),
        grid=(bn, ln // tm),
        in_specs=[
            tok(D_MODEL),
            pl.BlockSpec((6, 1, mod_rows, D_MODEL), mod_map),
            _const_spec((1, D_MODEL)),
            _const_spec((D_MODEL, ODD_IN_PAD)),
            pl.BlockSpec((tm, LANES), lambda b, t: (t, 0)),
            pl.BlockSpec((tm, LANES), lambda b, t: (t, 0)),
        ],
        out_specs=(head, head, tok(2 * KV_DIM), tok(4 * KV_DIM), col(6 * KV_DIM), col(4 * KV_DIM), tok(LANES)),
        compiler_params=_cparams("arbitrary", "arbitrary"),
        name="nsa_project",
    )(x, mods, g_mix, w_in, cos, sin)


def _compress_rows(load_rows, pe2_ref, w1_ref, w2e_ref, w2o_ref):
    n_chunk = 128
    lane = _lane_iota((n_chunk, LANES))
    low = lane < HEAD_DIM
    heads = [[] for _ in range(N_KV)]
    for p in range(CMP_STRIDE // 2):
        for m in range(N_PAIR):
            lo = load_rows(2 * p, m)
            hi = load_rows(2 * p + 1, m)
            heads[2 * m].append(jnp.where(low, lo, pltpu.roll(hi, HEAD_DIM, 1)))
            heads[2 * m + 1].append(jnp.where(low, pltpu.roll(lo, HEAD_DIM, 1), hi))
    x_all = jnp.concatenate([jnp.concatenate(hp, axis=1) for hp in heads], axis=0).astype(BF16)
    ab = _dot(x_all, w1_ref[...])
    pe = _dot(pe2_ref[...], w1_ref[...])
    hid0 = pe[0:1, :CMP_HIDDEN] + pe[1:2, CMP_HIDDEN:]
    hid = []
    for k in range(N_KV):
        a_part = ab[k * n_chunk:(k + 1) * n_chunk, :CMP_HIDDEN]
        b_part = ab[k * n_chunk:(k + 1) * n_chunk, CMP_HIDDEN:]
        nxt = pltpu.roll(b_part, n_chunk - 1, 0)
        hid.append(_gelu(a_part + nxt + hid0).astype(BF16))
    out = []
    for m in range(N_PAIR):
        out.append(_dot(hid[2 * m], w2e_ref[...]) + _dot(hid[2 * m + 1], w2o_ref[...]))
    return jnp.concatenate(out, axis=1)


def _compress_prompt_body(rowc_ref, pek_ref, w1k_ref, w2ke_ref, w2ko_ref, pev_ref, w1v_ref, w2ve_ref, w2vo_ref,
                          kc_ref, vc_ref):
    slabs = 2 * N_PAIR
    n_chunk = rowc_ref.shape[1] // (slabs * CMP_STRIDE)
    assert n_chunk == 128

    def rows(first):
        return lambda j, m: rowc_ref[0, pl.ds(slabs * j + first + m, n_chunk, stride=slabs * CMP_STRIDE), :]

    kc_ref[0] = _compress_rows(rows(0), pek_ref, w1k_ref, w2ke_ref, w2ko_ref).astype(BF16)
    vc_ref[0] = _compress_rows(rows(N_PAIR), pev_ref, w1v_ref, w2ve_ref, w2vo_ref).astype(BF16)


def _compress_prompt(rowc, cw_k, cw_v):
    bn, ln, w = rowc.shape
    rowc = rowc.reshape(bn, ln * w // LANES, LANES)
    wspecs = [_const_spec(a.shape) for a in cw_k + cw_v]
    out = pl.BlockSpec((1, 128, KV_DIM), lambda b: (b, 0, 0))
    return pl.pallas_call(
        _compress_prompt_body,
        out_shape=(jax.ShapeDtypeStruct((bn, 128, KV_DIM), BF16),) * 2,
        grid=(bn,),
        in_specs=[pl.BlockSpec((1,) + rowc.shape[1:], lambda b: (b, 0, 0))] + wspecs,
        out_specs=(out, out),
        compiler_params=_cparams("arbitrary"),
        name="compress_prompt",
    )(rowc, *cw_k, *cw_v)


def _flash_step(carry, s, vt):
    m_i, l_i, acc = carry
    m_new = jnp.maximum(m_i, jnp.max(s, -1, keepdims=True))
    alpha = jnp.exp(m_i - m_new)
    p = jnp.exp(s - m_new)
    l_new = alpha * l_i + jnp.sum(p, -1, keepdims=True)
    return m_new, l_new, alpha * acc + _dot_nt(p.astype(BF16), vt)


def _attn_prompt_body(qs_ref, qrs_ref, g_ref, kc_ref, vc_ref, kvt_ref, ov_ref, nege_ref, o_ref, *, t):
    qi = pl.program_id(1)
    q0 = pl.multiple_of(qi * t, t)
    pos = q0 + lax.broadcasted_iota(jnp.int32, (t, 1), 0)
    pos4 = jnp.concatenate([pos] * GROUP, axis=0)
    lane = _lane_iota((1, LANES))
    gates = g_ref[0]
    rows4 = GROUP * t

    cmask = (lane < LANES - 1) & (lane * CMP_STRIDE + (CMP_BLOCK - 1) <= pos4)
    o_cmp = []
    imp = jnp.zeros((t, LANES), F32)
    for k in range(N_KV):
        m = k // 2
        qs = qs_ref[0, GROUP * k:GROUP * (k + 1)].reshape(rows4, LANES)
        s = _dot_nt(qs, kc_ref[0, :, m * LANES:(m + 1) * LANES])
        p = _masked_softmax(s, cmask)
        o_cmp.append(_dot(p.astype(BF16), vc_ref[0, :, m * LANES:(m + 1) * LANES]))
        psum = p[0:t] + p[t:2 * t] + p[2 * t:3 * t] + p[3 * t:4 * t]
        imp = imp + jnp.dot(psum, ov_ref[k], preferred_element_type=F32, precision=HIGHEST)
    unsel = 1.0 - _select_blocks(imp, pos, N_SEL)
    unsel4 = jnp.concatenate([unsel] * GROUP, axis=0).astype(BF16)

    r = lax.broadcasted_iota(jnp.int32, (t, t), 0)
    c = lax.broadcasted_iota(jnp.int32, (t, t), 1)
    bias_diag = jnp.where(c <= r, 0.0, MASK_BIAS)
    bias_edge = jnp.where(c >= r, 0.0, MASK_BIAS)

    def add_bias(s, bias):
        return (s.reshape(GROUP, t, t) + bias[None]).reshape(rows4, t)

    def tile(stream, m, kt):
        k0 = pl.multiple_of(kt * t, t)
        base = stream * KV_DIM + m * LANES
        return kvt_ref[0, base:base + LANES, pl.ds(k0, t)]

    def init():
        return (jnp.full((rows4, 1), NEG, F32), jnp.zeros((rows4, 1), F32), jnp.zeros((rows4, LANES), F32))

    def finish(carry):
        _, l_f, acc = carry
        return acc / jnp.maximum(l_f, 1e-30)

    for k in range(N_KV):
        m = k // 2
        par = k % 2
        qrs = qrs_ref[0, GROUP * k:GROUP * (k + 1)].reshape(rows4, LANES)

        def sel_scores(kt, m=m, k=k, qrs=qrs):
            k0 = pl.multiple_of(kt * t, t)
            return _dot(qrs, tile(0, m, kt)) + _dot(unsel4, nege_ref[k, :, pl.ds(k0, t)])

        def sel_step(kt, cr, m=m, sel_scores=sel_scores):
            return _flash_step(cr, sel_scores(kt), tile(1, m, kt))

        carry = lax.fori_loop(0, qi, sel_step, init())
        carry = _flash_step(carry, add_bias(sel_scores(qi), bias_diag), tile(1, m, qi))
        o_sel = finish(carry)

        carry = init()
        for back, bias in ((2, bias_edge), (1, None)):
            def visit(cr, back=back, bias=bias, m=m, qrs=qrs):
                s = _dot(qrs, tile(2, m, qi - back))
                return _flash_step(cr, s if bias is None else add_bias(s, bias), tile(3, m, qi - back))
            carry = lax.cond(qi >= back, visit, lambda cr: cr, carry)
        carry = _flash_step(carry, add_bias(_dot(qrs, tile(2, m, qi)), bias_diag), tile(3, m, qi))
        o_win = finish(carry)

        blocks = []
        for g in range(GROUP):
            h = GROUP * k + g
            rows = slice(g * t, (g + 1) * t)
            o = (gates[:, 3 * h:3 * h + 1] * o_cmp[k][rows] + gates[:, 3 * h + 1:3 * h + 2] * o_sel[rows]
                 + gates[:, 3 * h + 2:3 * h + 3] * o_win[rows])
            if g % 2 != par:
                o = pltpu.roll(o, HEAD_DIM, 1)
            blocks.append(o)
        low = _lane_iota((t, LANES)) < HEAD_DIM
        for cc in range(GROUP // 2):
            col = (GROUP * k) // 2 + cc
            o_ref[0, :, col * LANES:(col + 1) * LANES] = jnp.where(low, blocks[2 * cc], blocks[2 * cc + 1]).astype(BF16)


def _attn_prompt(qs, qrs, gates, kc, vc, kvt, ov, nege, *, t):
    bn, _, ln, _ = qs.shape
    assert ln % t == 0 and ln <= SEL_LANES * SEL_BLOCK and WINDOW == 2 * t
    head = pl.BlockSpec((1, N_HEADS, t, LANES), lambda b, i: (b, 0, i, 0))
    cmp_spec = pl.BlockSpec((1, 128, KV_DIM), lambda b, i: (b, 0, 0))
    return pl.pallas_call(
        functools.partial(_attn_prompt_body, t=t),
        out_shape=jax.ShapeDtypeStruct((bn, ln, D_MODEL), BF16),
        grid=(bn, ln // t),
        in_specs=[head, head, pl.BlockSpec((1, t, LANES), lambda b, i: (b, i, 0)), cmp_spec, cmp_spec,
                  pl.BlockSpec((1, 4 * KV_DIM, ln), lambda b, i: (b, 0, 0)),
                  _const_spec(ov.shape), _const_spec(nege.shape)],
        out_specs=pl.BlockSpec((1, t, D_MODEL), lambda b, i: (b, i, 0)),
        compiler_params=_cparams("arbitrary", "arbitrary"),
        name="attn_prompt",
    )(qs, qrs, gates, kc, vc, kvt, ov, nege)


def _attn_sample_body(pt_ref, qs_ref, qrs_ref, g_ref, new_ref, wkc_ref, wvc_ref, wks_ref, wvs_ref, *rest, n_pages):
    del pt_ref
    pools = [rest[i * n_pages:(i + 1) * n_pages] for i in range(4)]
    (pek_ref, w1k_ref, w2ke_ref, w2ko_ref, pev_ref, w1v_ref, w2ve_ref, w2vo_ref, ov_ref,
     o_ref, wko_ref, wvo_ref, x_ref) = rest[4 * n_pages:]
    past = n_pages * PAGE_SIZE

    def compress(pages, *weights):
        for i, pg in enumerate(pages):
            for m in range(N_PAIR):
                x_ref[m, i * PAGE_SIZE:(i + 1) * PAGE_SIZE, :] = pg[0, 0, m * LANES:(m + 1) * LANES, :].T
        load = lambda j, m: x_ref[m, pl.ds(j, past // CMP_STRIDE, stride=CMP_STRIDE), :]
        return _compress_rows(load, *weights).astype(BF16)

    kc = compress(pools[0], pek_ref, w1k_ref, w2ke_ref, w2ko_ref)
    vc = compress(pools[1], pev_ref, w1v_ref, w2ve_ref, w2vo_ref)

    lane = _lane_iota((SUBLANES, LANES))
    upper_rows = lax.broadcasted_iota(jnp.int32, (SUBLANES, LANES), 0) >= GROUP
    gates = g_ref[0]
    pos = jnp.full((SUBLANES, 1), past, jnp.int32)

    cmask = lane < LANES - 1
    o_cmp = []
    imp = jnp.zeros((SUBLANES, LANES), F32)
    for m in range(N_PAIR):
        qs = qs_ref[0, 2 * GROUP * m:2 * GROUP * (m + 1), :]
        p = _masked_softmax(_dot_nt(qs, kc[:, m * LANES:(m + 1) * LANES]), cmask)
        o_cmp.append(_dot(p.astype(BF16), vc[:, m * LANES:(m + 1) * LANES]))
        for kk in range(2):
            psum = jnp.sum(p[kk * GROUP:(kk + 1) * GROUP], axis=0, keepdims=True)
            imp = imp + jnp.dot(jnp.broadcast_to(psum, (SUBLANES, LANES)), ov_ref[2 * m + kk],
                                preferred_element_type=F32, precision=HIGHEST)
    sel = _select_blocks(imp, pos, N_SEL - 1)
    seg = lane >> SEL_LANES_SHIFT
    expand = _block_expand(past)

    def new_row(stream, m):
        base = stream * KV_DIM + m * LANES
        return new_ref[0, :, base:base + LANES]

    for m in range(N_PAIR):
        rows = slice(m * LANES, (m + 1) * LANES)
        qrs = qrs_ref[0, 2 * GROUP * m:2 * GROUP * (m + 1), :]
        qrf = qrs.astype(F32)
        want = jnp.where(upper_rows, 2 * m + 1, 2 * m)
        sel_rows = jnp.where(seg == want, sel, 0.0).astype(BF16)
        ok = _dot(sel_rows, expand) > 0.5
        s = jnp.concatenate([_dot(qrs, pg[0, 0, rows, :].astype(BF16)) for pg in pools[2]], axis=1)
        s_new = jnp.sum(qrf * new_row(0, m), -1, keepdims=True)
        s = jnp.where(ok, s, NEG)
        mx = jnp.maximum(jnp.max(s, -1, keepdims=True), s_new)
        p = jnp.where(ok, jnp.exp(s - mx), 0.0)
        p_new = jnp.exp(s_new - mx)
        denom = jnp.maximum(jnp.sum(p, -1, keepdims=True) + p_new, 1e-30)
        pb = p.astype(BF16)
        acc = p_new * new_row(1, m)
        for i, pg in enumerate(pools[3]):
            acc = acc + _dot_nt(pb[:, i * PAGE_SIZE:(i + 1) * PAGE_SIZE], pg[0, 0, rows, :].astype(BF16))
        o_sel = acc / denom
        sw = _dot(qrs, wks_ref[0, 0, rows, :].astype(BF16))
        sw_new = jnp.sum(qrf * new_row(2, m), -1, keepdims=True)
        mw = jnp.maximum(jnp.max(sw, -1, keepdims=True), sw_new)
        pw = jnp.exp(sw - mw)
        pw_new = jnp.exp(sw_new - mw)
        dw = jnp.maximum(jnp.sum(pw, -1, keepdims=True) + pw_new, 1e-30)
        o_win = (_dot_nt(pw.astype(BF16), wvs_ref[0, 0, rows, :].astype(BF16)) + pw_new * new_row(3, m)) / dw
        for kk in range(2):
            for c in range(GROUP // 2):
                halves = []
                for gg in range(2):
                    g = 2 * c + gg
                    h = GROUP * (2 * m + kk) + g
                    r = kk * GROUP + g
                    o = (gates[:, 3 * h:3 * h + 1] * o_cmp[m][r:r + 1] + gates[:, 3 * h + 1:3 * h + 2] * o_sel[r:r + 1]
                         + gates[:, 3 * h + 2:3 * h + 3] * o_win[r:r + 1])
                    if gg != kk:
                        o = pltpu.roll(jnp.broadcast_to(o, (SUBLANES, LANES)), HEAD_DIM, 1)[0:1]
                    halves.append(o)
                col = (GROUP * (2 * m + kk)) // 2 + c
                o_ref[0, :, col * LANES:(col + 1) * LANES] = jnp.where(
                    _lane_iota((1, LANES)) < HEAD_DIM, halves[0], halves[1]).astype(BF16)

    nbuf = wks_ref.shape[-1]
    last = _lane_iota((KV_DIM, nbuf)) == nbuf - 1
    wko_ref[0] = jnp.where(last, wkc_ref[0], pltpu.roll(wks_ref[0, 0], nbuf - 1, 1))
    wvo_ref[0] = jnp.where(last, wvc_ref[0], pltpu.roll(wvs_ref[0, 0], nbuf - 1, 1))


def _attn_sample(page_table, layer, qs, qrs, gates, new_rows, wk_col, wv_col, win_k, win_v,
                 pool_ck, pool_cv, pool_sk, pool_sv, cw_k, cw_v, ov):
    sn = qs.shape[0]
    n_pages = page_table.shape[1]
    nbuf = win_k.shape[-1]
    past = n_pages * PAGE_SIZE
    assert nbuf == WINDOW and past == SEL_LANES * SEL_BLOCK
    per_seq = lambda *blk: pl.BlockSpec((1,) + blk, lambda s, pt: (s,) + (0,) * len(blk))
    page_map = lambda s, pt, p: (layer, pt[s, p], 0, 0)
    page_specs = [pl.BlockSpec((1, 1, KV_DIM, PAGE_SIZE), functools.partial(page_map, p=p)) for p in range(n_pages)]
    win_spec = pl.BlockSpec((1, 1, KV_DIM, nbuf), lambda s, pt: (layer, s, 0, 0))
    weights = cw_k + cw_v + (ov,)
    grid_spec = pltpu.PrefetchScalarGridSpec(
        num_scalar_prefetch=1,
        grid=(sn,),
        in_specs=[per_seq(N_HEADS, LANES), per_seq(N_HEADS, LANES), per_seq(1, LANES), per_seq(1, 4 * KV_DIM),
                  per_seq(KV_DIM, 1), per_seq(KV_DIM, 1), win_spec, win_spec]
        + page_specs * 4
        + [pl.BlockSpec(w.shape, functools.partial(lambda s, pt, n: (0,) * n, n=w.ndim)) for w in weights],
        out_specs=(per_seq(1, D_MODEL), per_seq(KV_DIM, nbuf), per_seq(KV_DIM, nbuf)),
        scratch_shapes=[pltpu.VMEM((N_PAIR, past, LANES), F32)],
    )
    pools = [pool_ck] * n_pages + [pool_cv] * n_pages + [pool_sk] * n_pages + [pool_sv] * n_pages
    return pl.pallas_call(
        functools.partial(_attn_sample_body, n_pages=n_pages),
        out_shape=(jax.ShapeDtypeStruct((sn, 1, D_MODEL), BF16),
                   jax.ShapeDtypeStruct((sn, KV_DIM, nbuf), F32),
                   jax.ShapeDtypeStruct((sn, KV_DIM, nbuf), F32)),
        grid_spec=grid_spec,
        compiler_params=_cparams("arbitrary"),
        name="attn_sample",
    )(page_table, qs, qrs, gates, new_rows, wk_col, wv_col, win_k, win_v, *pools, *weights)


def _rope_tables(pos):
    half = HEAD_DIM // 2
    inv = jnp.power(jnp.float32(ROPE_THETA), -jnp.arange(half, dtype=F32) * (2.0 / HEAD_DIM))
    ang = pos.astype(F32)[:, None] * inv[None, :]
    cos, sin = jnp.cos(ang), jnp.sin(ang)
    return jnp.concatenate([cos, cos, cos, cos], -1), jnp.concatenate([-sin, sin, -sin, sin], -1)


def _overlap_tables():
    n_cmp, n_sel = LANES - 1, SEL_LANES
    i = np.arange(n_cmp)[:, None] * CMP_STRIDE
    j = np.arange(n_sel)[None, :] * SEL_BLOCK
    ov = ((i < j + SEL_BLOCK) & (i + CMP_BLOCK > j)).astype(np.float32)
    out = np.zeros((N_KV, LANES, LANES), np.float32)
    for k in range(N_KV):
        out[k, :n_cmp, k * SEL_LANES:(k + 1) * SEL_LANES] = ov
    return jnp.asarray(out)


def _unselected_bias_table(ln):
    r = np.arange(LANES)[:, None]
    c = np.arange(ln)[None, :]
    out = np.zeros((N_KV, LANES, ln), np.float32)
    for k in range(N_KV):
        out[k] = np.where((r // SEL_LANES == k) & (r % SEL_LANES == c // SEL_BLOCK), MASK_BIAS, 0.0)
    return jnp.asarray(out, dtype=BF16)


def _rows_last(t):
    nd = t.ndim
    t = jnp.transpose(t, tuple(range(nd - 3)) + (nd - 2, nd - 1, nd - 3))
    return t.reshape(t.shape[:-3] + (KV_DIM, t.shape[-1]))


def _rows_first(t):
    nd = t.ndim
    t = t.reshape(t.shape[:-2] + (N_KV, HEAD_DIM, t.shape[-1]))
    return jnp.transpose(t, tuple(range(nd - 2)) + (nd, nd - 2, nd - 1))


def _compress_weights(pe, w1, w2):
    r = CMP_BLOCK // CMP_STRIDE
    flat = CMP_STRIDE * HEAD_DIM
    w1cat = jnp.concatenate([w1[m * CMP_STRIDE:(m + 1) * CMP_STRIDE].reshape(flat, CMP_HIDDEN) for m in range(r)], 1)
    pe2 = jnp.zeros((SUBLANES, flat), F32).at[:r].set(pe.reshape(r, flat))
    zeros = jnp.zeros_like(w2)
    return (pe2.astype(BF16), w1cat.astype(BF16), jnp.concatenate([w2, zeros], 1).astype(BF16),
            jnp.concatenate([zeros, w2], 1).astype(BF16))


def kernel(x_prompt, x_sample, state_conv, cache_cmp_k, cache_cmp_v, cache_sel_k, cache_sel_v, state_win_k, state_win_v, page_table, c_prompt, c_sample, ada_w, ada_b, norm_mix_g, norm_ffn_g, ffn_w1, ffn_w2, even_w_in, even_w_out, conv_w, conv_b, conv_ln_g, conv_ln_b, sgu_ln_g, sgu_ln_b, sgu_w, sgu_b, odd_w_in, odd_w_out, cmp_pe_k, cmp_w1_k, cmp_w2_k, cmp_pe_v, cmp_w1_v, cmp_w2_v, final_norm_g):
    depth = ada_w.shape[0]
    bn, ln, _ = x_prompt.shape
    sn = x_sample.shape[0]
    past = page_table.shape[1] * PAGE_SIZE

    ada = _ada_all(jnp.concatenate([c_prompt, c_sample], 0), ada_w, ada_b)
    ov = _overlap_tables()
    nege = _unselected_bias_table(ln)
    cos_p, sin_p = _rope_tables(jnp.arange(ln))
    cos_s, sin_s = _rope_tables(jnp.full((sn,), past))
    g_fin = final_norm_g.reshape(1, D_MODEL)
    row = lambda a: a.reshape(1, -1)
    pools = [_rows_last(c) for c in (cache_cmp_k, cache_cmp_v, cache_sel_k, cache_sel_v)]
    win_k_all, win_v_all = _rows_last(state_win_k), _rows_last(state_win_v)

    xp = x_prompt
    xs = x_sample.reshape(1, sn, D_MODEL)
    conv_p, conv_s, chv_p, chv_s = [], [], [], []
    cols_p, cols_s, win_s = [], [], []
    for l in range(depth):
        mods_p = ada[l, :, :bn].reshape(6, bn, 1, D_MODEL)
        mods_s = ada[l, :, bn:].reshape(6, 1, sn, D_MODEL)
        g_mix = row(norm_mix_g[l])
        if l % 2 == 0:
            e = l // 2
            w_in = even_w_in[e].astype(BF16)
            w_out = even_w_out[e].astype(BF16)
            cln = (row(conv_b[e]), row(conv_ln_g[e]), row(conv_ln_b[e]), row(sgu_ln_g[e]), row(sgu_ln_b[e]))
            yp, bp, vp = _even_prompt(xp, mods_p, g_mix, w_in, conv_w[e], *cln, sgu_w[e], sgu_b[e].T, tm=256)
            w00 = row(jnp.repeat(sgu_w[e, :, 0, 0], B_GROUP_DIM))
            b0 = row(jnp.repeat(sgu_b[e, :, 0], B_GROUP_DIM))
            ys, a_s, v_s = _even_sample(xs[0], mods_s[:, 0], g_mix, w_in, jnp.swapaxes(state_conv[e], 0, 1),
                                        conv_w[e], *cln, w00, b0)
            conv_p.append(bp)
            conv_s.append(jnp.concatenate([state_conv[e][:, 1:], a_s[:, None, :]], axis=1))
            chv_p.append(vp)
            chv_s.append(v_s[:, None, :])
            ys = ys[None]
        else:
            o = l // 2
            w_in = jnp.pad(odd_w_in[o], ((0, 0), (0, ODD_IN_PAD - odd_w_in.shape[-1]))).astype(BF16)
            w_out = odd_w_out[o].astype(BF16)
            cw_k = _compress_weights(cmp_pe_k[o], cmp_w1_k[o], cmp_w2_k[o])
            cw_v = _compress_weights(cmp_pe_v[o], cmp_w1_v[o], cmp_w2_v[o])
            qs, qrs, rowc, _, cols, colsb, gates = _nsa_project(xp, mods_p, g_mix, w_in, cos_p, sin_p, tm=256)
            kc, vc = _compress_prompt(rowc, cw_k, cw_v)
            yp = _attn_prompt(qs, qrs, gates, kc, vc, colsb, ov, nege, t=256)
            cols_p.append(cols)
            qs, qrs, _, rows, cols, _, gates = _nsa_project(xs, mods_s, g_mix, w_in, cos_s, sin_s, tm=sn)
            heads = lambda t: jnp.swapaxes(t[0], 0, 1)
            col = lambda i: rows[0, :, i * KV_DIM:(i + 1) * KV_DIM].reshape(sn, KV_DIM, 1)
            ys, wk_new, wv_new = _attn_sample(
                page_table, o, heads(qs), heads(qrs), gates.reshape(sn, 1, LANES), rows.reshape(sn, 1, 4 * KV_DIM),
                col(2), col(3), win_k_all, win_v_all, *pools, cw_k, cw_v, ov)
            ys = ys.reshape(1, sn, D_MODEL)
            cols_s.append(cols[0])
            win_s.append((wk_new, wv_new))
        final = l == depth - 1
        w1 = ffn_w1[l].astype(BF16)
        w2 = ffn_w2[l].astype(BF16)
        g_ffn = row(norm_ffn_g[l])
        xp = _post(xp, yp, mods_p, g_ffn, w_out, w1, w2, g_fin, final=final, tm=512)
        xs = _post(xs, ys, mods_s, g_ffn, w_out, w1, w2, g_fin, final=final, tm=sn)
    st = lambda lst: jnp.stack(lst, axis=0)
    nw = min(WINDOW, ln)
    cols_p = st(cols_p)
    stream_p = lambda i, lo: _rows_first(cols_p[:, :, i * KV_DIM:(i + 1) * KV_DIM, lo:])
    cols_s = st(cols_s)
    stream_s = lambda i: _rows_first(cols_s[:, i * KV_DIM:(i + 1) * KV_DIM, :])[:, :, None]
    return (xp, xs.reshape(sn, 1, D_MODEL),
            st(conv_p), st(conv_s), st(chv_p), st(chv_s),
            stream_p(0, 0), stream_p(1, 0), stream_p(2, 0), stream_p(3, 0), stream_p(4, ln - nw), stream_p(5, ln - nw),
            stream_s(0), stream_s(1), stream_s(2), stream_s(3),
            _rows_first(st([w[0] for w in win_s])), _rows_first(st([w[1] for w in win_s])))
```

```python
import functools

import numpy as np
import jax
import jax.numpy as jnp
from jax import lax
from jax.experimental import pallas as pl
from jax.experimental.pallas import tpu as pltpu

F32 = jnp.float32
BF16 = jnp.bfloat16
HIGHEST = lax.Precision.HIGHEST

LANES = 128
SUBLANES = 8
VMEM_LIMIT_BYTES = 56 * 1024 * 1024

D_MODEL = 1024
D_FF = 4 * D_MODEL
D_A = D_MODEL // 2
CONV_WIDTH = 31
CONV_HALO = 32
D_B = D_MODEL // 2
B_GROUPS = 4
B_GROUP_DIM = D_B // B_GROUPS
CHUNK = 128
N_HEADS = 16
HEAD_DIM = 64
N_KV = 4
GROUP = N_HEADS // N_KV
KV_DIM = N_KV * HEAD_DIM
N_PAIR = KV_DIM // LANES
CMP_BLOCK = 32
CMP_STRIDE = 16
CMP_HIDDEN = 2 * HEAD_DIM
SEL_BLOCK = 64
N_SEL = 8
SEL_LANES = 32
SEL_SHIFT = 6
SEL_LANES_SHIFT = 5
WINDOW = 512
PAGE_SIZE = 128
ROPE_THETA = 10000.0
EPS = 1e-6
NEG = -1e30
MASK_BIAS = -(2.0 ** 100)
FORCED = 1e4
SCALE = HEAD_DIM ** -0.5
ODD_IN_PAD = N_HEADS * HEAD_DIM + 6 * KV_DIM + LANES


def _cparams(*sem):
    return pltpu.CompilerParams(dimension_semantics=sem, vmem_limit_bytes=VMEM_LIMIT_BYTES)


def _const_spec(shape):
    n = len(shape)
    return pl.BlockSpec(shape, lambda *_: (0,) * n)


def _sigmoid(x):
    return 1.0 / (1.0 + jnp.exp(-x))


def _silu(x):
    return x * _sigmoid(x)


def _gelu(x):
    return 0.5 * x * (1.0 + jnp.tanh(np.sqrt(2.0 / np.pi).astype(np.float32) * (x + 0.044715 * (x * x * x))))


def _rmsnorm(x, g):
    return x * lax.rsqrt(jnp.mean(x * x, -1, keepdims=True) + EPS) * g


def _layernorm(x, g, b):
    xc = x - jnp.mean(x, -1, keepdims=True)
    return xc * lax.rsqrt(jnp.mean(xc * xc, -1, keepdims=True) + EPS) * g + b


def _modulate(x, g, shift, scale):
    return _rmsnorm(x, g) * (1.0 + scale) + shift


def _dot(a, b):
    return jnp.dot(a, b, preferred_element_type=F32)


def _dot_nt(a, b):
    return lax.dot_general(a, b, (((1,), (1,)), ((), ())), preferred_element_type=F32)


def _lane_iota(shape):
    return lax.broadcasted_iota(jnp.int32, shape, len(shape) - 1)


def _masked_softmax(s, mask):
    s = jnp.where(mask, s, NEG)
    m = jnp.max(s, -1, keepdims=True)
    p = jnp.where(mask, jnp.exp(s - m), 0.0)
    return p / jnp.maximum(jnp.sum(p, -1, keepdims=True), 1e-30)


def _rope_pair(xs, cos, sin_signed):
    half = HEAD_DIM // 2
    first = (_lane_iota(xs.shape) & (HEAD_DIM - 1)) < half
    swapped = jnp.where(first, pltpu.roll(xs, LANES - half, 1), pltpu.roll(xs, half, 1))
    return xs * cos + swapped * sin_signed


def _select_blocks(imp, pos, n_top):
    lane = _lane_iota(imp.shape)
    j = lane & (SEL_LANES - 1)
    cur = pos >> SEL_SHIFT
    valid = j * SEL_BLOCK <= pos
    forced = (j == 0) | (j == cur) | (j == cur - 1)
    score = jnp.where(valid, jnp.where(forced, FORCED, imp), NEG)
    rank = jnp.zeros(imp.shape, F32)
    for d in range(1, SEL_LANES):
        nonwrap = j >= d
        rot = jnp.where(nonwrap, pltpu.roll(score, d, 1), pltpu.roll(score, (d - SEL_LANES) % LANES, 1))
        ahead = (rot > score) | ((rot == score) & nonwrap)
        rank = rank + jnp.where(ahead, 1.0, 0.0)
    return jnp.where(rank < n_top, 1.0, 0.0)


def _ada_body(c_ref, w_ref, b_ref, o_ref):
    a = _silu(c_ref[...])
    o_ref[0, 0] = jnp.dot(a, w_ref[0], preferred_element_type=F32, precision=HIGHEST) + b_ref[0, 0]


def _ada_all(c_all, ada_w, ada_b):
    depth = ada_w.shape[0]
    rows = c_all.shape[0]
    return pl.pallas_call(
        _ada_body,
        out_shape=jax.ShapeDtypeStruct((depth, 6, rows, D_MODEL), F32),
        grid=(depth, 6),
        in_specs=[
            _const_spec((rows, D_MODEL)),
            pl.BlockSpec((1, D_MODEL, D_MODEL), lambda l, j: (l, 0, j)),
            pl.BlockSpec((1, 1, 1, D_MODEL), lambda l, j: (l, j, 0, 0)),
        ],
        out_specs=pl.BlockSpec((1, 1, rows, D_MODEL), lambda l, j: (l, j, 0, 0)),
        compiler_params=_cparams("arbitrary", "arbitrary"),
        name="ada_params",
    )(c_all, ada_w, ada_b.reshape(depth, 6, 1, D_MODEL))


def _post_body(x_ref, y_ref, mod_ref, gffn_ref, wo_ref, w1_ref, w2_ref, gfin_ref, o_ref, *, final, ff_chunk):
    x = x_ref[0]
    x = x + mod_ref[2, 0] * _dot(y_ref[0], wo_ref[...])
    h = _modulate(x, gffn_ref[...], mod_ref[3, 0], mod_ref[4, 0]).astype(BF16)
    acc = jnp.zeros(x.shape, F32)
    for c in range(D_FF // ff_chunk):
        t = _dot(h, w1_ref[:, c * ff_chunk:(c + 1) * ff_chunk])
        t = jnp.square(jnp.maximum(t, 0.0)).astype(BF16)
        acc = acc + _dot(t, w2_ref[c * ff_chunk:(c + 1) * ff_chunk, :])
    x = x + mod_ref[5, 0] * acc
    if final:
        x = _rmsnorm(x, gfin_ref[...])
    o_ref[0] = x


def _post(x, y, mods, g_ffn, w_out, w1, w2, g_fin, *, final, tm):
    bn, ln, _ = x.shape
    d_in = y.shape[-1]
    r = mods.shape[2]
    tm = min(tm, ln)
    mod_rows = 1 if r == 1 else tm
    mod_map = (lambda b, t: (0, b, 0, 0)) if r == 1 else (lambda b, t: (0, b, t, 0))
    return pl.pallas_call(
        functools.partial(_post_body, final=final, ff_chunk=1024),
        out_shape=jax.ShapeDtypeStruct(x.shape, F32),
        grid=(bn, ln // tm),
        in_specs=[
            pl.BlockSpec((1, tm, D_MODEL), lambda b, t: (b, t, 0)),
            pl.BlockSpec((1, tm, d_in), lambda b, t: (b, t, 0)),
            pl.BlockSpec((6, 1, mod_rows, D_MODEL), mod_map),
            _const_spec((1, D_MODEL)),
            _const_spec((d_in, D_MODEL)),
            _const_spec((D_MODEL, D_FF)),
            _const_spec((D_FF, D_MODEL)),
            _const_spec((1, D_MODEL)),
        ],
        out_specs=pl.BlockSpec((1, tm, D_MODEL), lambda b, t: (b, t, 0)),
        compiler_params=_cparams("arbitrary", "arbitrary"),
        name="post_mlp",
    )(x, y, mods, g_ffn, w_out, w1, w2, g_fin)


def _even_prompt_body(x_ref, mod_ref, gmix_ref, win_ref, cw_ref, cb_ref, clg_ref, clb_ref, slg_ref, slb_ref,
                      sw_ref, sbt_ref, y_ref, conv_ref, chv_ref, ext_ref, sh_ref, *, tm, conv_rows):
    t = pl.program_id(1)
    h = _modulate(x_ref[0], gmix_ref[...], mod_ref[0, 0], mod_ref[1, 0]).astype(BF16)
    z = _dot(h, win_ref[...])
    a = z[:, :D_A] * _sigmoid(z[:, D_A:2 * D_A])
    u = _gelu(z[:, 2 * D_A:2 * D_A + D_B])
    v = _layernorm(_gelu(z[:, 2 * D_A + D_B:]), slg_ref[...], slb_ref[...])

    @pl.when(t == 0)
    def _():
        ext_ref[0:CONV_HALO, :] = jnp.zeros((CONV_HALO, D_A), F32)

    ext_ref[CONV_HALO:CONV_HALO + tm, :] = a
    first = CONV_HALO - (CONV_WIDTH - 1)
    span = sh_ref.shape[1]
    for sft in range(1, SUBLANES):
        sh_ref[sft - 1] = ext_ref[sft:sft + span, :]
    pieces = []
    for c in range(tm // conv_rows):
        acc = jnp.zeros((conv_rows, D_A), F32) + cb_ref[...]
        for w in range(CONV_WIDTH):
            sft = (first + w) % SUBLANES
            r0 = first + w - sft + c * conv_rows
            src = ext_ref[r0:r0 + conv_rows, :] if sft == 0 else sh_ref[sft - 1, r0:r0 + conv_rows, :]
            acc = acc + src * cw_ref[w:w + 1, :]
        pieces.append(acc)
    a_conv = jnp.concatenate(pieces, axis=0)
    conv_ref[0] = ext_ref[tm + first:tm + CONV_HALO, :]
    ext_ref[0:CONV_HALO, :] = ext_ref[tm:tm + CONV_HALO, :]
    a_out = _silu(_layernorm(a_conv, clg_ref[...], clb_ref[...]))

    row = lax.broadcasted_iota(jnp.int32, (CHUNK, CHUNK), 0)
    col = lax.broadcasted_iota(jnp.int32, (CHUNK, CHUNK), 1)
    wm = [jnp.where(col <= row, sw_ref[g], 0.0).astype(BF16) for g in range(B_GROUPS)]
    vb = v.astype(BF16)
    gate_rows = []
    for c in range(tm // CHUNK):
        cols = []
        for g in range(B_GROUPS):
            vg = vb[c * CHUNK:(c + 1) * CHUNK, g * B_GROUP_DIM:(g + 1) * B_GROUP_DIM]
            cols.append(_dot(wm[g], vg) + sbt_ref[:, g:g + 1])
        gate_rows.append(jnp.concatenate(cols, axis=1))
    b_out = u * jnp.concatenate(gate_rows, axis=0)

    y_ref[0, :, :D_A] = a_out.astype(BF16)
    y_ref[0, :, D_A:] = b_out.astype(BF16)
    chv_ref[0] = v[tm - CHUNK:, :]


def _even_prompt(x, mods, g_mix, w_in, conv_w, conv_b, cln_g, cln_b, sln_g, sln_b, sgu_w, sgu_bt, *, tm):
    bn, ln, _ = x.shape
    assert ln % tm == 0 and tm % CHUNK == 0 and ln >= CONV_HALO
    return pl.pallas_call(
        functools.partial(_even_prompt_body, tm=tm, conv_rows=64),
        out_shape=(
            jax.ShapeDtypeStruct((bn, ln, D_MODEL), BF16),
            jax.ShapeDtypeStruct((bn, CONV_WIDTH - 1, D_A), F32),
            jax.ShapeDtypeStruct((bn, CHUNK, D_B), F32),
        ),
        grid=(bn, ln // tm),
        in_specs=[
            pl.BlockSpec((1, tm, D_MODEL), lambda b, t: (b, t, 0)),
            pl.BlockSpec((6, 1, 1, D_MODEL), lambda b, t: (0, b, 0, 0)),
            _const_spec((1, D_MODEL)),
            _const_spec((D_MODEL, 2 * D_A + 2 * D_B)),
            _const_spec((CONV_WIDTH, D_A)),
            _const_spec((1, D_A)), _const_spec((1, D_A)), _const_spec((1, D_A)),
            _const_spec((1, D_B)), _const_spec((1, D_B)),
            _const_spec((B_GROUPS, CHUNK, CHUNK)),
            _const_spec((CHUNK, B_GROUPS)),
        ],
        out_specs=(
            pl.BlockSpec((1, tm, D_MODEL), lambda b, t: (b, t, 0)),
            pl.BlockSpec((1, CONV_WIDTH - 1, D_A), lambda b, t: (b, 0, 0)),
            pl.BlockSpec((1, CHUNK, D_B), lambda b, t: (b, 0, 0)),
        ),
        scratch_shapes=[pltpu.VMEM((tm + CONV_HALO, D_A), F32),
                        pltpu.VMEM((SUBLANES - 1, tm + CONV_HALO - SUBLANES, D_A), F32)],
        compiler_params=_cparams("arbitrary", "arbitrary"),
        name="even_prompt",
    )(x, mods, g_mix, w_in, conv_w, conv_b, cln_g, cln_b, sln_g, sln_b, sgu_w, sgu_bt)


def _even_sample_body(x_ref, mod_ref, gmix_ref, win_ref, st_ref, cw_ref, cb_ref, clg_ref, clb_ref, slg_ref,
                      slb_ref, w00_ref, b0_ref, y_ref, a_ref, v_ref):
    h = _modulate(x_ref[...], gmix_ref[...], mod_ref[0], mod_ref[1]).astype(BF16)
    z = _dot(h, win_ref[...])
    a = z[:, :D_A] * _sigmoid(z[:, D_A:2 * D_A])
    u = _gelu(z[:, 2 * D_A:2 * D_A + D_B])
    v = _layernorm(_gelu(z[:, 2 * D_A + D_B:]), slg_ref[...], slb_ref[...])
    acc = a * cw_ref[CONV_WIDTH - 1:CONV_WIDTH, :] + cb_ref[...]
    for w in range(CONV_WIDTH - 1):
        acc = acc + st_ref[w] * cw_ref[w:w + 1, :]
    a_out = _silu(_layernorm(acc, clg_ref[...], clb_ref[...]))
    b_out = u * (w00_ref[...] * v + b0_ref[...])
    y_ref[:, :D_A] = a_out.astype(BF16)
    y_ref[:, D_A:] = b_out.astype(BF16)
    a_ref[...] = a
    v_ref[...] = v


def _even_sample(x, mods, g_mix, w_in, state_t, conv_w, conv_b, cln_g, cln_b, sln_g, sln_b, w00, b0):
    sn = x.shape[0]
    shapes = [a.shape for a in (x, mods, g_mix, w_in, state_t, conv_w, conv_b, cln_g, cln_b, sln_g, sln_b, w00, b0)]
    return pl.pallas_call(
        _even_sample_body,
        out_shape=(
            jax.ShapeDtypeStruct((sn, D_MODEL), BF16),
            jax.ShapeDtypeStruct((sn, D_A), F32),
            jax.ShapeDtypeStruct((sn, D_B), F32),
        ),
        grid=(1,),
        in_specs=[_const_spec(s) for s in shapes],
        out_specs=(_const_spec((sn, D_MODEL)), _const_spec((sn, D_A)), _const_spec((sn, D_B))),
        compiler_params=_cparams("arbitrary"),
        name="even_sample",
    )(x, mods, g_mix, w_in, state_t, conv_w, conv_b, cln_g, cln_b, sln_g, sln_b, w00, b0)


def _proj_body(x_ref, mod_ref, gmix_ref, w_ref, cos_ref, sin_ref,
               qs_ref, qrs_ref, rowc_ref, rows_ref, cols_ref, colsb_ref, g_ref):
    tm = x_ref.shape[1]
    h = _modulate(x_ref[0], gmix_ref[...], mod_ref[0, 0], mod_ref[1, 0]).astype(BF16)
    z = _dot(h, w_ref[...])
    cos = cos_ref[...]
    sin = sin_ref[...]
    q0 = N_HEADS * HEAD_DIM
    low = _lane_iota((tm, LANES)) < HEAD_DIM
    for c in range(q0 // LANES):
        xs = z[:, c * LANES:(c + 1) * LANES] * SCALE
        xr = _rope_pair(xs, cos, sin)
        for half in range(2):
            hd = 2 * c + half
            par = (hd // GROUP) % 2
            keep = low if par == 0 else jnp.logical_not(low)
            for src, ref in ((xs, qs_ref), (xr, qrs_ref)):
                v = src if half == par else pltpu.roll(src, HEAD_DIM, 1)
                ref[0, hd] = jnp.where(keep, v, 0.0).astype(BF16)

    def kv(i):
        return z[:, q0 + i * KV_DIM:q0 + (i + 1) * KV_DIM]

    def kv_rot(i):
        base = q0 + i * KV_DIM
        return jnp.concatenate(
            [_rope_pair(z[:, base + c * LANES:base + (c + 1) * LANES], cos, sin) for c in range(N_PAIR)], axis=1)

    rows = jnp.concatenate([kv(0), kv(1), kv_rot(2), kv(3), kv_rot(4), kv(5)], axis=1)
    rowc_ref[0] = rows[:, :2 * KV_DIM]
    rows_ref[0] = rows[:, 2 * KV_DIM:]
    cols = rows.T
    cols_ref[0] = cols
    colsb_ref[0] = cols[2 * KV_DIM:].astype(BF16)
    g_ref[0] = _sigmoid(z[:, q0 + 6 * KV_DIM:])


def _nsa_project(x, mods, g_mix, w_in, cos, sin, *, tm):
    bn, ln, _ = x.shape
    r = mods.shape[2]
    tm = min(tm, ln)
    mod_rows = 1 if r == 1 else tm
    mod_map = (lambda b, t: (0, b, 0, 0)) if r == 1 else (lambda b, t: (0, b, t, 0))
    tok = lambda w: pl.BlockSpec((1, tm, w), lambda b, t: (b, t, 0))
    col = lambda w: pl.BlockSpec((1, w, tm), lambda b, t: (b, 0, t))
    head = pl.BlockSpec((1, N_HEADS, tm, LANES), lambda b, t: (b, 0, t, 0))
    return pl.pallas_call(
        _proj_body,
        out_shape=(
            jax.ShapeDtypeStruct((bn, N_HEADS, ln, LANES), BF16),
            jax.ShapeDtypeStruct((bn, N_HEADS, ln, LANES), BF16),
            jax.ShapeDtypeStruct((bn, ln, 2 * KV_DIM), F32),
            jax.ShapeDtypeStruct((bn, ln, 4 * KV_DIM), F32),
            jax.ShapeDtypeStruct((bn, 6 * KV_DIM, ln), F32),
            jax.ShapeDtypeStruct((bn, 4 * KV_DIM, ln), BF16),
            jax.ShapeDtypeStruct((bn, ln, LANES), F32),
        ),
        grid=(bn, ln // tm),
        in_specs=[
            tok(D_MODEL),
            pl.BlockSpec((6, 1, mod_rows, D_MODEL), mod_map),
            _const_spec((1, D_MODEL)),
            _const_spec((D_MODEL, ODD_IN_PAD)),
            pl.BlockSpec((tm, LANES), lambda b, t: (t, 0)),
            pl.BlockSpec((tm, LANES), lambda b, t: (t, 0)),
        ],
        out_specs=(head, head, tok(2 * KV_DIM), tok(4 * KV_DIM), col(6 * KV_DIM), col(4 * KV_DIM), tok(LANES)),
        compiler_params=_cparams("arbitrary", "arbitrary"),
        name="nsa_project",
    )(x, mods, g_mix, w_in, cos, sin)


def _compress_rows(load_rows, pe2_ref, w1_ref, w2e_ref, w2o_ref):
    n_chunk = 128
    lane = _lane_iota((n_chunk, LANES))
    low = lane < HEAD_DIM
    heads = [[] for _ in range(N_KV)]
    for p in range(CMP_STRIDE // 2):
        for m in range(N_PAIR):
            lo = load_rows(2 * p, m)
            hi = load_rows(2 * p + 1, m)
            heads[2 * m].append(jnp.where(low, lo, pltpu.roll(hi, HEAD_DIM, 1)))
            heads[2 * m + 1].append(jnp.where(low, pltpu.roll(lo, HEAD_DIM, 1), hi))
    x_all = jnp.concatenate([jnp.concatenate(hp, axis=1) for hp in heads], axis=0).astype(BF16)
    ab = _dot(x_all, w1_ref[...])
    pe = _dot(pe2_ref[...], w1_ref[...])
    hid0 = pe[0:1, :CMP_HIDDEN] + pe[1:2, CMP_HIDDEN:]
    hid = []
    for k in range(N_KV):
        a_part = ab[k * n_chunk:(k + 1) * n_chunk, :CMP_HIDDEN]
        b_part = ab[k * n_chunk:(k + 1) * n_chunk, CMP_HIDDEN:]
        nxt = pltpu.roll(b_part, n_chunk - 1, 0)
        hid.append(_gelu(a_part + nxt + hid0).astype(BF16))
    out = []
    for m in range(N_PAIR):
        out.append(_dot(hid[2 * m], w2e_ref[...]) + _dot(hid[2 * m + 1], w2o_ref[...]))
    return jnp.concatenate(out, axis=1)


def _compress_prompt_body(rowc_ref, pek_ref, w1k_ref, w2ke_ref, w2ko_ref, pev_ref, w1v_ref, w2ve_ref, w2vo_ref,
                          kc_ref, vc_ref):
    slabs = 2 * N_PAIR
    n_chunk = rowc_ref.shape[1] // (slabs * CMP_STRIDE)
    assert n_chunk == 128

    def rows(first):
        return lambda j, m: rowc_ref[0, pl.ds(slabs * j + first + m, n_chunk, stride=slabs * CMP_STRIDE), :]

    kc_ref[0] = _compress_rows(rows(0), pek_ref, w1k_ref, w2ke_ref, w2ko_ref).astype(BF16)
    vc_ref[0] = _compress_rows(rows(N_PAIR), pev_ref, w1v_ref, w2ve_ref, w2vo_ref).astype(BF16)


def _compress_prompt(rowc, cw_k, cw_v):
    bn, ln, w = rowc.shape
    rowc = rowc.reshape(bn, ln * w // LANES, LANES)
    wspecs = [_const_spec(a.shape) for a in cw_k + cw_v]
    out = pl.BlockSpec((1, 128, KV_DIM), lambda b: (b, 0, 0))
    return pl.pallas_call(
        _compress_prompt_body,
        out_shape=(jax.ShapeDtypeStruct((bn, 128, KV_DIM), BF16),) * 2,
        grid=(bn,),
        in_specs=[pl.BlockSpec((1,) + rowc.shape[1:], lambda b: (b, 0, 0))] + wspecs,
        out_specs=(out, out),
        compiler_params=_cparams("arbitrary"),
        name="compress_prompt",
    )(rowc, *cw_k, *cw_v)


def _flash_step(carry, s, vt_ones):
    m_i, acc = carry
    m_new = jnp.maximum(m_i, jnp.max(s, -1, keepdims=True))
    alpha = jnp.exp(m_i - m_new)
    p = jnp.exp(s - m_new)
    return m_new, alpha * acc + _dot_nt(p.astype(BF16), vt_ones)


def _attn_prompt_body(qs_ref, qrs_ref, g_ref, kc_ref, vc_ref, kvt_ref, ov_ref, nege_ref, o_ref, *, t):
    qi = pl.program_id(1)
    q0 = pl.multiple_of(qi * t, t)
    pos = q0 + lax.broadcasted_iota(jnp.int32, (t, 1), 0)
    pos4 = jnp.concatenate([pos] * GROUP, axis=0)
    lane = _lane_iota((1, LANES))
    gates = g_ref[0]
    rows4 = GROUP * t

    cmask = (lane < LANES - 1) & (lane * CMP_STRIDE + (CMP_BLOCK - 1) <= pos4)
    o_cmp = []
    imp = jnp.zeros((t, LANES), F32)
    for k in range(N_KV):
        m = k // 2
        qs = qs_ref[0, GROUP * k:GROUP * (k + 1)].reshape(rows4, LANES)
        s = _dot_nt(qs, kc_ref[0, :, m * LANES:(m + 1) * LANES])
        p = _masked_softmax(s, cmask)
        o_cmp.append(_dot(p.astype(BF16), vc_ref[0, :, m * LANES:(m + 1) * LANES]))
        psum = p[0:t] + p[t:2 * t] + p[2 * t:3 * t] + p[3 * t:4 * t]
        imp = imp + jnp.dot(psum, ov_ref[k], preferred_element_type=F32, precision=HIGHEST)
    unsel = 1.0 - _select_blocks(imp, pos, N_SEL)
    unsel4 = jnp.concatenate([unsel] * GROUP, axis=0).astype(BF16)
    ones = jnp.ones((LANES, t), BF16)

    r = lax.broadcasted_iota(jnp.int32, (t, t), 0)
    c = lax.broadcasted_iota(jnp.int32, (t, t), 1)
    bias_diag = jnp.where(c <= r, 0.0, MASK_BIAS)
    bias_edge = jnp.where(c >= r, 0.0, MASK_BIAS)

    def add_bias(s, bias):
        return (s.reshape(GROUP, t, t) + bias[None]).reshape(rows4, t)

    def tile(stream, m, kt):
        k0 = pl.multiple_of(kt * t, t)
        base = stream * KV_DIM + m * LANES
        return kvt_ref[0, base:base + LANES, pl.ds(k0, t)]

    def v_ones(stream, m, kt):
        return jnp.concatenate([tile(stream, m, kt), ones], axis=0)

    def init():
        return jnp.full((rows4, 1), NEG, F32), jnp.zeros((rows4, 2 * LANES), F32)

    def finish(carry):
        _, acc = carry
        return acc[:, :LANES] / jnp.maximum(acc[:, LANES:], 1e-30)

    for k in range(N_KV):
        m = k // 2
        par = k % 2
        qrs = qrs_ref[0, GROUP * k:GROUP * (k + 1)].reshape(rows4, LANES)

        qu = jnp.concatenate([qrs, unsel4], axis=1)

        def sel_scores(kt, m=m, k=k, qu=qu):
            k0 = pl.multiple_of(kt * t, t)
            return _dot(qu, jnp.concatenate([tile(0, m, kt), nege_ref[k, :, pl.ds(k0, t)]], axis=0))

        def sel_step(kt, cr, m=m, sel_scores=sel_scores):
            return _flash_step(cr, sel_scores(kt), v_ones(1, m, kt))

        carry = lax.fori_loop(0, qi, sel_step, init())
        carry = _flash_step(carry, add_bias(sel_scores(qi), bias_diag), v_ones(1, m, qi))
        o_sel = finish(carry)

        carry = init()
        for back, bias in ((2, bias_edge), (1, None)):
            def visit(cr, back=back, bias=bias, m=m, qrs=qrs):
                s = _dot(qrs, tile(2, m, qi - back))
                return _flash_step(cr, s if bias is None else add_bias(s, bias), v_ones(3, m, qi - back))
            carry = lax.cond(qi >= back, visit, lambda cr: cr, carry)
        carry = _flash_step(carry, add_bias(_dot(qrs, tile(2, m, qi)), bias_diag), v_ones(3, m, qi))
        o_win = finish(carry)

        blocks = []
        for g in range(GROUP):
            h = GROUP * k + g
            rows = slice(g * t, (g + 1) * t)
            o = (gates[:, 3 * h:3 * h + 1] * o_cmp[k][rows] + gates[:, 3 * h + 1:3 * h + 2] * o_sel[rows]
                 + gates[:, 3 * h + 2:3 * h + 3] * o_win[rows])
            if g % 2 != par:
                o = pltpu.roll(o, HEAD_DIM, 1)
            blocks.append(o)
        low = _lane_iota((t, LANES)) < HEAD_DIM
        for cc in range(GROUP // 2):
            col = (GROUP * k) // 2 + cc
            o_ref[0, :, col * LANES:(col + 1) * LANES] = jnp.where(low, blocks[2 * cc], blocks[2 * cc + 1]).astype(BF16)


def _attn_prompt(qs, qrs, gates, kc, vc, kvt, ov, nege, *, t):
    bn, _, ln, _ = qs.shape
    assert ln % t == 0 and ln <= SEL_LANES * SEL_BLOCK and WINDOW == 2 * t
    head = pl.BlockSpec((1, N_HEADS, t, LANES), lambda b, i: (b, 0, i, 0))
    cmp_spec = pl.BlockSpec((1, 128, KV_DIM), lambda b, i: (b, 0, 0))
    return pl.pallas_call(
        functools.partial(_attn_prompt_body, t=t),
        out_shape=jax.ShapeDtypeStruct((bn, ln, D_MODEL), BF16),
        grid=(bn, ln // t),
        in_specs=[head, head, pl.BlockSpec((1, t, LANES), lambda b, i: (b, i, 0)), cmp_spec, cmp_spec,
                  pl.BlockSpec((1, 4 * KV_DIM, ln), lambda b, i: (b, 0, 0)),
                  _const_spec(ov.shape), _const_spec(nege.shape)],
        out_specs=pl.BlockSpec((1, t, D_MODEL), lambda b, i: (b, i, 0)),
        compiler_params=_cparams("arbitrary", "arbitrary"),
        name="attn_prompt",
    )(qs, qrs, gates, kc, vc, kvt, ov, nege)


def _attn_sample_body(pt_ref, qs_ref, qrs_ref, g_ref, new_ref, wkc_ref, wvc_ref, wks_ref, wvs_ref, *rest, n_pages):
    del pt_ref
    pools = [rest[i * n_pages:(i + 1) * n_pages] for i in range(4)]
    (pek_ref, w1k_ref, w2ke_ref, w2ko_ref, pev_ref, w1v_ref, w2ve_ref, w2vo_ref, ov_ref, expand_ref,
     o_ref, wko_ref, wvo_ref, x_ref) = rest[4 * n_pages:]
    past = n_pages * PAGE_SIZE

    def compress(pages, *weights):
        for i, pg in enumerate(pages):
            for m in range(N_PAIR):
                x_ref[m, i * PAGE_SIZE:(i + 1) * PAGE_SIZE, :] = pg[0, 0, m * LANES:(m + 1) * LANES, :].T
        load = lambda j, m: x_ref[m, pl.ds(j, past // CMP_STRIDE, stride=CMP_STRIDE), :]
        return _compress_rows(load, *weights).astype(BF16)

    kc = compress(pools[0], pek_ref, w1k_ref, w2ke_ref, w2ko_ref)
    vc = compress(pools[1], pev_ref, w1v_ref, w2ve_ref, w2vo_ref)

    lane = _lane_iota((SUBLANES, LANES))
    upper_rows = lax.broadcasted_iota(jnp.int32, (SUBLANES, LANES), 0) >= GROUP
    gates = g_ref[0]
    pos = jnp.full((SUBLANES, 1), past, jnp.int32)

    cmask = lane < LANES - 1
    o_cmp = []
    imp = jnp.zeros((SUBLANES, LANES), F32)
    for m in range(N_PAIR):
        qs = qs_ref[0, 2 * GROUP * m:2 * GROUP * (m + 1), :]
        p = _masked_softmax(_dot_nt(qs, kc[:, m * LANES:(m + 1) * LANES]), cmask)
        o_cmp.append(_dot(p.astype(BF16), vc[:, m * LANES:(m + 1) * LANES]))
        for kk in range(2):
            psum = jnp.sum(p[kk * GROUP:(kk + 1) * GROUP], axis=0, keepdims=True)
            imp = imp + jnp.dot(jnp.broadcast_to(psum, (SUBLANES, LANES)), ov_ref[2 * m + kk],
                                preferred_element_type=F32, precision=HIGHEST)
    sel = _select_blocks(imp, pos, N_SEL - 1)
    seg = lane >> SEL_LANES_SHIFT
    expand = expand_ref[...]

    def new_row(stream, m):
        base = stream * KV_DIM + m * LANES
        return new_ref[0, :, base:base + LANES]

    for m in range(N_PAIR):
        rows = slice(m * LANES, (m + 1) * LANES)
        qrs = qrs_ref[0, 2 * GROUP * m:2 * GROUP * (m + 1), :]
        qrf = qrs.astype(F32)
        want = jnp.where(upper_rows, 2 * m + 1, 2 * m)
        sel_rows = jnp.where(seg == want, sel, 0.0).astype(BF16)
        ok = _dot(sel_rows, expand) > 0.5
        s = jnp.concatenate([_dot(qrs, pg[0, 0, rows, :].astype(BF16)) for pg in pools[2]], axis=1)
        s_new = jnp.sum(qrf * new_row(0, m), -1, keepdims=True)
        s = jnp.where(ok, s, NEG)
        mx = jnp.maximum(jnp.max(s, -1, keepdims=True), s_new)
        p = jnp.where(ok, jnp.exp(s - mx), 0.0)
        p_new = jnp.exp(s_new - mx)
        denom = jnp.maximum(jnp.sum(p, -1, keepdims=True) + p_new, 1e-30)
        pb = p.astype(BF16)
        acc = p_new * new_row(1, m)
        for i, pg in enumerate(pools[3]):
            acc = acc + _dot_nt(pb[:, i * PAGE_SIZE:(i + 1) * PAGE_SIZE], pg[0, 0, rows, :].astype(BF16))
        o_sel = acc / denom
        sw = _dot(qrs, wks_ref[0, 0, rows, :].astype(BF16))
        sw_new = jnp.sum(qrf * new_row(2, m), -1, keepdims=True)
        mw = jnp.maximum(jnp.max(sw, -1, keepdims=True), sw_new)
        pw = jnp.exp(sw - mw)
        pw_new = jnp.exp(sw_new - mw)
        dw = jnp.maximum(jnp.sum(pw, -1, keepdims=True) + pw_new, 1e-30)
        o_win = (_dot_nt(pw.astype(BF16), wvs_ref[0, 0, rows, :].astype(BF16)) + pw_new * new_row(3, m)) / dw
        for kk in range(2):
            for c in range(GROUP // 2):
                halves = []
                for gg in range(2):
                    g = 2 * c + gg
                    h = GROUP * (2 * m + kk) + g
                    r = kk * GROUP + g
                    o = (gates[:, 3 * h:3 * h + 1] * o_cmp[m][r:r + 1] + gates[:, 3 * h + 1:3 * h + 2] * o_sel[r:r + 1]
                         + gates[:, 3 * h + 2:3 * h + 3] * o_win[r:r + 1])
                    if gg != kk:
                        o = pltpu.roll(jnp.broadcast_to(o, (SUBLANES, LANES)), HEAD_DIM, 1)[0:1]
                    halves.append(o)
                col = (GROUP * (2 * m + kk)) // 2 + c
                o_ref[0, :, col * LANES:(col + 1) * LANES] = jnp.where(
                    _lane_iota((1, LANES)) < HEAD_DIM, halves[0], halves[1]).astype(BF16)

    nbuf = wks_ref.shape[-1]
    last = _lane_iota((KV_DIM, nbuf)) == nbuf - 1
    wko_ref[0] = jnp.where(last, wkc_ref[0], pltpu.roll(wks_ref[0, 0], nbuf - 1, 1))
    wvo_ref[0] = jnp.where(last, wvc_ref[0], pltpu.roll(wvs_ref[0, 0], nbuf - 1, 1))


def _attn_sample(page_table, layer, qs, qrs, gates, new_rows, wk_col, wv_col, win_k, win_v,
                 pool_ck, pool_cv, pool_sk, pool_sv, cw_k, cw_v, ov):
    sn = qs.shape[0]
    n_pages = page_table.shape[1]
    nbuf = win_k.shape[-1]
    past = n_pages * PAGE_SIZE
    assert nbuf == WINDOW and past == SEL_LANES * SEL_BLOCK
    per_seq = lambda *blk: pl.BlockSpec((1,) + blk, lambda s, pt: (s,) + (0,) * len(blk))
    page_map = lambda s, pt, p: (layer, pt[s, p], 0, 0)
    page_specs = [pl.BlockSpec((1, 1, KV_DIM, PAGE_SIZE), functools.partial(page_map, p=p)) for p in range(n_pages)]
    win_spec = pl.BlockSpec((1, 1, KV_DIM, nbuf), lambda s, pt: (layer, s, 0, 0))
    expand = (np.arange(LANES)[:, None] % SEL_LANES) == (np.arange(past)[None, :] // SEL_BLOCK)
    weights = cw_k + cw_v + (ov, jnp.asarray(expand, dtype=BF16))
    grid_spec = pltpu.PrefetchScalarGridSpec(
        num_scalar_prefetch=1,
        grid=(sn,),
        in_specs=[per_seq(N_HEADS, LANES), per_seq(N_HEADS, LANES), per_seq(1, LANES), per_seq(1, 4 * KV_DIM),
                  per_seq(KV_DIM, 1), per_seq(KV_DIM, 1), win_spec, win_spec]
        + page_specs * 4
        + [pl.BlockSpec(w.shape, functools.partial(lambda s, pt, n: (0,) * n, n=w.ndim)) for w in weights],
        out_specs=(per_seq(1, D_MODEL), per_seq(KV_DIM, nbuf), per_seq(KV_DIM, nbuf)),
        scratch_shapes=[pltpu.VMEM((N_PAIR, past, LANES), F32)],
    )
    pools = [pool_ck] * n_pages + [pool_cv] * n_pages + [pool_sk] * n_pages + [pool_sv] * n_pages
    return pl.pallas_call(
        functools.partial(_attn_sample_body, n_pages=n_pages),
        out_shape=(jax.ShapeDtypeStruct((sn, 1, D_MODEL), BF16),
                   jax.ShapeDtypeStruct((sn, KV_DIM, nbuf), F32),
                   jax.ShapeDtypeStruct((sn, KV_DIM, nbuf), F32)),
        grid_spec=grid_spec,
        compiler_params=_cparams("arbitrary"),
        name="attn_sample",
    )(page_table, qs, qrs, gates, new_rows, wk_col, wv_col, win_k, win_v, *pools, *weights)


def _rope_tables(pos):
    half = HEAD_DIM // 2
    inv = jnp.power(jnp.float32(ROPE_THETA), -jnp.arange(half, dtype=F32) * (2.0 / HEAD_DIM))
    ang = pos.astype(F32)[:, None] * inv[None, :]
    cos, sin = jnp.cos(ang), jnp.sin(ang)
    return jnp.concatenate([cos, cos, cos, cos], -1), jnp.concatenate([-sin, sin, -sin, sin], -1)


def _overlap_tables():
    n_cmp, n_sel = LANES - 1, SEL_LANES
    i = np.arange(n_cmp)[:, None] * CMP_STRIDE
    j = np.arange(n_sel)[None, :] * SEL_BLOCK
    ov = ((i < j + SEL_BLOCK) & (i + CMP_BLOCK > j)).astype(np.float32)
    out = np.zeros((N_KV, LANES, LANES), np.float32)
    for k in range(N_KV):
        out[k, :n_cmp, k * SEL_LANES:(k + 1) * SEL_LANES] = ov
    return jnp.asarray(out)


def _unselected_bias_table(ln):
    r = np.arange(LANES)[:, None]
    c = np.arange(ln)[None, :]
    out = np.zeros((N_KV, LANES, ln), np.float32)
    for k in range(N_KV):
        out[k] = np.where((r // SEL_LANES == k) & (r % SEL_LANES == c // SEL_BLOCK), MASK_BIAS, 0.0)
    return jnp.asarray(out, dtype=BF16)


def _rows_last(t):
    nd = t.ndim
    t = jnp.transpose(t, tuple(range(nd - 3)) + (nd - 2, nd - 1, nd - 3))
    return t.reshape(t.shape[:-3] + (KV_DIM, t.shape[-1]))


def _rows_first(t):
    nd = t.ndim
    t = t.reshape(t.shape[:-2] + (N_KV, HEAD_DIM, t.shape[-1]))
    return jnp.transpose(t, tuple(range(nd - 2)) + (nd, nd - 2, nd - 1))


def _compress_weights(pe, w1, w2):
    r = CMP_BLOCK // CMP_STRIDE
    flat = CMP_STRIDE * HEAD_DIM
    w1cat = jnp.concatenate([w1[m * CMP_STRIDE:(m + 1) * CMP_STRIDE].reshape(flat, CMP_HIDDEN) for m in range(r)], 1)
    pe2 = jnp.zeros((SUBLANES, flat), F32).at[:r].set(pe.reshape(r, flat))
    zeros = jnp.zeros_like(w2)
    return (pe2.astype(BF16), w1cat.astype(BF16), jnp.concatenate([w2, zeros], 1).astype(BF16),
            jnp.concatenate([zeros, w2], 1).astype(BF16))


def kernel(x_prompt, x_sample, state_conv, cache_cmp_k, cache_cmp_v, cache_sel_k, cache_sel_v, state_win_k, state_win_v, page_table, c_prompt, c_sample, ada_w, ada_b, norm_mix_g, norm_ffn_g, ffn_w1, ffn_w2, even_w_in, even_w_out, conv_w, conv_b, conv_ln_g, conv_ln_b, sgu_ln_g, sgu_ln_b, sgu_w, sgu_b, odd_w_in, odd_w_out, cmp_pe_k, cmp_w1_k, cmp_w2_k, cmp_pe_v, cmp_w1_v, cmp_w2_v, final_norm_g):
    depth = ada_w.shape[0]
    bn, ln, _ = x_prompt.shape
    sn = x_sample.shape[0]
    past = page_table.shape[1] * PAGE_SIZE

    ada = _ada_all(jnp.concatenate([c_prompt, c_sample], 0), ada_w, ada_b)
    ov = _overlap_tables()
    nege = _unselected_bias_table(ln)
    cos_p, sin_p = _rope_tables(jnp.arange(ln))
    cos_s, sin_s = _rope_tables(jnp.full((sn,), past))
    g_fin = final_norm_g.reshape(1, D_MODEL)
    row = lambda a: a.reshape(1, -1)
    pools = [_rows_last(c) for c in (cache_cmp_k, cache_cmp_v, cache_sel_k, cache_sel_v)]
    win_k_all, win_v_all = _rows_last(state_win_k), _rows_last(state_win_v)

    xp = x_prompt
    xs = x_sample.reshape(1, sn, D_MODEL)
    conv_p, conv_s, chv_p, chv_s = [], [], [], []
    cols_p, cols_s, win_s = [], [], []
    for l in range(depth):
        mods_p = ada[l, :, :bn].reshape(6, bn, 1, D_MODEL)
        mods_s = ada[l, :, bn:].reshape(6, 1, sn, D_MODEL)
        g_mix = row(norm_mix_g[l])
        if l % 2 == 0:
            e = l // 2
            w_in = even_w_in[e].astype(BF16)
            w_out = even_w_out[e].astype(BF16)
            cln = (row(conv_b[e]), row(conv_ln_g[e]), row(conv_ln_b[e]), row(sgu_ln_g[e]), row(sgu_ln_b[e]))
            yp, bp, vp = _even_prompt(xp, mods_p, g_mix, w_in, conv_w[e], *cln, sgu_w[e], sgu_b[e].T, tm=256)
            w00 = row(jnp.repeat(sgu_w[e, :, 0, 0], B_GROUP_DIM))
            b0 = row(jnp.repeat(sgu_b[e, :, 0], B_GROUP_DIM))
            ys, a_s, v_s = _even_sample(xs[0], mods_s[:, 0], g_mix, w_in, jnp.swapaxes(state_conv[e], 0, 1),
                                        conv_w[e], *cln, w00, b0)
            conv_p.append(bp)
            conv_s.append(jnp.concatenate([state_conv[e][:, 1:], a_s[:, None, :]], axis=1))
            chv_p.append(vp)
            chv_s.append(v_s[:, None, :])
            ys = ys[None]
        else:
            o = l // 2
            w_in = jnp.pad(odd_w_in[o], ((0, 0), (0, ODD_IN_PAD - odd_w_in.shape[-1]))).astype(BF16)
            w_out = odd_w_out[o].astype(BF16)
            cw_k = _compress_weights(cmp_pe_k[o], cmp_w1_k[o], cmp_w2_k[o])
            cw_v = _compress_weights(cmp_pe_v[o], cmp_w1_v[o], cmp_w2_v[o])
            qs, qrs, rowc, _, cols, colsb, gates = _nsa_project(xp, mods_p, g_mix, w_in, cos_p, sin_p, tm=256)
            kc, vc = _compress_prompt(rowc, cw_k, cw_v)
            yp = _attn_prompt(qs, qrs, gates, kc, vc, colsb, ov, nege, t=256)
            cols_p.append(cols)
            qs, qrs, _, rows, cols, _, gates = _nsa_project(xs, mods_s, g_mix, w_in, cos_s, sin_s, tm=sn)
            heads = lambda t: jnp.swapaxes(t[0], 0, 1)
            col = lambda i: rows[0, :, i * KV_DIM:(i + 1) * KV_DIM].reshape(sn, KV_DIM, 1)
            ys, wk_new, wv_new = _attn_sample(
                page_table, o, heads(qs), heads(qrs), gates.reshape(sn, 1, LANES), rows.reshape(sn, 1, 4 * KV_DIM),
                col(2), col(3), win_k_all, win_v_all, *pools, cw_k, cw_v, ov)
            ys = ys.reshape(1, sn, D_MODEL)
            cols_s.append(cols[0])
            win_s.append((wk_new, wv_new))
        final = l == depth - 1
        w1 = ffn_w1[l].astype(BF16)
        w2 = ffn_w2[l].astype(BF16)
        g_ffn = row(norm_ffn_g[l])
        xp = _post(xp, yp, mods_p, g_ffn, w_out, w1, w2, g_fin, final=final, tm=512)
        xs = _post(xs, ys, mods_s, g_ffn, w_out, w1, w2, g_fin, final=final, tm=sn)
    st = lambda lst: jnp.stack(lst, axis=0)
    nw = min(WINDOW, ln)
    cols_p = st(cols_p)
    stream_p = lambda i, lo: _rows_first(cols_p[:, :, i * KV_DIM:(i + 1) * KV_DIM, lo:])
    cols_s = st(cols_s)
    stream_s = lambda i: _rows_first(cols_s[:, i * KV_DIM:(i + 1) * KV_DIM, :])[:, :, None]
    return (xp, xs.reshape(sn, 1, D_MODEL),
            st(conv_p), st(conv_s), st(chv_p), st(chv_s),
            stream_p(0, 0), stream_p(1, 0), stream_p(2, 0), stream_p(3, 0), stream_p(4, ln - nw), stream_p(5, ln - nw),
            stream_s(0), stream_s(1), stream_s(2), stream_s(3),
            _rows_first(st([w[0] for w in win_s])), _rows_first(st([w[1] for w in win_s])))
```

```python
import functools

import numpy as np
import jax
import jax.numpy as jnp
from jax import lax
from jax.experimental import pallas as pl
from jax.experimental.pallas import tpu as pltpu

F32 = jnp.float32
BF16 = jnp.bfloat16
HIGHEST = lax.Precision.HIGHEST

LANES = 128
SUBLANES = 8
VMEM_LIMIT_BYTES = 56 * 1024 * 1024

D_MODEL = 1024
D_FF = 4 * D_MODEL
D_A = D_MODEL // 2
CONV_WIDTH = 31
CONV_HALO = 32
D_B = D_MODEL // 2
B_GROUPS = 4
B_GROUP_DIM = D_B // B_GROUPS
CHUNK = 128
N_HEADS = 16
HEAD_DIM = 64
N_KV = 4
GROUP = N_HEADS // N_KV
KV_DIM = N_KV * HEAD_DIM
N_PAIR = KV_DIM // LANES
CMP_BLOCK = 32
CMP_STRIDE = 16
CMP_HIDDEN = 2 * HEAD_DIM
SEL_BLOCK = 64
N_SEL = 8
SEL_LANES = 32
SEL_SHIFT = 6
SEL_LANES_SHIFT = 5
WINDOW = 512
PAGE_SIZE = 128
ROPE_THETA = 10000.0
EPS = 1e-6
NEG = -1e30
MASK_BIAS = -(2.0 ** 100)
FORCED = 1e4
SCALE = HEAD_DIM ** -0.5
ODD_IN_PAD = N_HEADS * HEAD_DIM + 6 * KV_DIM + LANES


def _cparams(*sem):
    return pltpu.CompilerParams(dimension_semantics=sem, vmem_limit_bytes=VMEM_LIMIT_BYTES)


def _const_spec(shape):
    n = len(shape)
    return pl.BlockSpec(shape, lambda *_: (0,) * n)


def _sigmoid(x):
    return 1.0 / (1.0 + jnp.exp(-x))


def _silu(x):
    return x * _sigmoid(x)


def _gelu(x):
    return 0.5 * x * (1.0 + jnp.tanh(np.sqrt(2.0 / np.pi).astype(np.float32) * (x + 0.044715 * (x * x * x))))


def _rmsnorm(x, g):
    return x * lax.rsqrt(jnp.mean(x * x, -1, keepdims=True) + EPS) * g


def _layernorm(x, g, b):
    xc = x - jnp.mean(x, -1, keepdims=True)
    return xc * lax.rsqrt(jnp.mean(xc * xc, -1, keepdims=True) + EPS) * g + b


def _modulate(x, g, shift, scale):
    return _rmsnorm(x, g) * (1.0 + scale) + shift


def _dot(a, b):
    return jnp.dot(a, b, preferred_element_type=F32)


def _dot_nt(a, b):
    return lax.dot_general(a, b, (((1,), (1,)), ((), ())), preferred_element_type=F32)


def _lane_iota(shape):
    return lax.broadcasted_iota(jnp.int32, shape, len(shape) - 1)


def _masked_softmax(s, mask):
    s = jnp.where(mask, s, NEG)
    m = jnp.max(s, -1, keepdims=True)
    p = jnp.where(mask, jnp.exp(s - m), 0.0)
    return p / jnp.maximum(jnp.sum(p, -1, keepdims=True), 1e-30)


def _rope_pair(xs, cos, sin_signed):
    half = HEAD_DIM // 2
    first = (_lane_iota(xs.shape) & (HEAD_DIM - 1)) < half
    swapped = jnp.where(first, pltpu.roll(xs, LANES - half, 1), pltpu.roll(xs, half, 1))
    return xs * cos + swapped * sin_signed


def _select_blocks(imp, pos, n_top):
    lane = _lane_iota(imp.shape)
    j = lane & (SEL_LANES - 1)
    cur = pos >> SEL_SHIFT
    valid = j * SEL_BLOCK <= pos
    forced = (j == 0) | (j == cur) | (j == cur - 1)
    score = jnp.where(valid, jnp.where(forced, FORCED, imp), NEG)
    rank = jnp.zeros(imp.shape, F32)
    for d in range(1, SEL_LANES):
        nonwrap = j >= d
        rot = jnp.where(nonwrap, pltpu.roll(score, d, 1), pltpu.roll(score, (d - SEL_LANES) % LANES, 1))
        ahead = (rot > score) | ((rot == score) & nonwrap)
        rank = rank + jnp.where(ahead, 1.0, 0.0)
    return jnp.where(rank < n_top, 1.0, 0.0)


def _ada_body(c_ref, w_ref, b_ref, o_ref):
    a = _silu(c_ref[...])
    o_ref[0, 0] = jnp.dot(a, w_ref[0], preferred_element_type=F32, precision=HIGHEST) + b_ref[0, 0]


def _ada_all(c_all, ada_w, ada_b):
    depth = ada_w.shape[0]
    rows = c_all.shape[0]
    return pl.pallas_call(
        _ada_body,
        out_shape=jax.ShapeDtypeStruct((depth, 6, rows, D_MODEL), F32),
        grid=(depth, 6),
        in_specs=[
            _const_spec((rows, D_MODEL)),
            pl.BlockSpec((1, D_MODEL, D_MODEL), lambda l, j: (l, 0, j)),
            pl.BlockSpec((1, 1, 1, D_MODEL), lambda l, j: (l, j, 0, 0)),
        ],
        out_specs=pl.BlockSpec((1, 1, rows, D_MODEL), lambda l, j: (l, j, 0, 0)),
        compiler_params=_cparams("arbitrary", "arbitrary"),
        name="ada_params",
    )(c_all, ada_w, ada_b.reshape(depth, 6, 1, D_MODEL))


def _post_body(x_ref, y_ref, mod_ref, gffn_ref, wo_ref, w1_ref, w2_ref, gfin_ref, o_ref, *, final, ff_chunk):
    x = x_ref[0]
    x = x + mod_ref[2, 0] * _dot(y_ref[0], wo_ref[...])
    h = _modulate(x, gffn_ref[...], mod_ref[3, 0], mod_ref[4, 0]).astype(BF16)
    acc = jnp.zeros(x.shape, F32)
    for c in range(D_FF // ff_chunk):
        t = _dot(h, w1_ref[:, c * ff_chunk:(c + 1) * ff_chunk])
        t = jnp.square(jnp.maximum(t, 0.0)).astype(BF16)
        acc = acc + _dot(t, w2_ref[c * ff_chunk:(c + 1) * ff_chunk, :])
    x = x + mod_ref[5, 0] * acc
    if final:
        x = _rmsnorm(x, gfin_ref[...])
    o_ref[0] = x


def _post(x, y, mods, g_ffn, w_out, w1, w2, g_fin, *, final, tm):
    bn, ln, _ = x.shape
    d_in = y.shape[-1]
    r = mods.shape[2]
    tm = min(tm, ln)
    mod_rows = 1 if r == 1 else tm
    mod_map = (lambda b, t: (0, b, 0, 0)) if r == 1 else (lambda b, t: (0, b, t, 0))
    return pl.pallas_call(
        functools.partial(_post_body, final=final, ff_chunk=1024),
        out_shape=jax.ShapeDtypeStruct(x.shape, F32),
        grid=(bn, ln // tm),
        in_specs=[
            pl.BlockSpec((1, tm, D_MODEL), lambda b, t: (b, t, 0)),
            pl.BlockSpec((1, tm, d_in), lambda b, t: (b, t, 0)),
            pl.BlockSpec((6, 1, mod_rows, D_MODEL), mod_map),
            _const_spec((1, D_MODEL)),
            _const_spec((d_in, D_MODEL)),
            _const_spec((D_MODEL, D_FF)),
            _const_spec((D_FF, D_MODEL)),
            _const_spec((1, D_MODEL)),
        ],
        out_specs=pl.BlockSpec((1, tm, D_MODEL), lambda b, t: (b, t, 0)),
        compiler_params=_cparams("arbitrary", "arbitrary"),
        name="post_mlp",
    )(x, y, mods, g_ffn, w_out, w1, w2, g_fin)


def _even_prompt_body(x_ref, mod_ref, gmix_ref, win_ref, cw_ref, cb_ref, clg_ref, clb_ref, slg_ref, slb_ref,
                      sw_ref, sbt_ref, y_ref, conv_ref, chv_ref, ext_ref, sh_ref, *, tm, conv_rows):
    t = pl.program_id(1)
    h = _modulate(x_ref[0], gmix_ref[...], mod_ref[0, 0], mod_ref[1, 0]).astype(BF16)
    z = _dot(h, win_ref[...])
    a = z[:, :D_A] * _sigmoid(z[:, D_A:2 * D_A])
    u = _gelu(z[:, 2 * D_A:2 * D_A + D_B])
    v = _layernorm(_gelu(z[:, 2 * D_A + D_B:]), slg_ref[...], slb_ref[...])

    @pl.when(t == 0)
    def _():
        ext_ref[0:CONV_HALO, :] = jnp.zeros((CONV_HALO, D_A), F32)

    ext_ref[CONV_HALO:CONV_HALO + tm, :] = a
    first = CONV_HALO - (CONV_WIDTH - 1)
    span = sh_ref.shape[1]
    for sft in range(1, SUBLANES):
        sh_ref[sft - 1] = ext_ref[sft:sft + span, :]
    pieces = []
    for c in range(tm // conv_rows):
        acc = jnp.zeros((conv_rows, D_A), F32) + cb_ref[...]
        for w in range(CONV_WIDTH):
            sft = (first + w) % SUBLANES
            r0 = first + w - sft + c * conv_rows
            src = ext_ref[r0:r0 + conv_rows, :] if sft == 0 else sh_ref[sft - 1, r0:r0 + conv_rows, :]
            acc = acc + src * cw_ref[w:w + 1, :]
        pieces.append(acc)
    a_conv = jnp.concatenate(pieces, axis=0)
    conv_ref[0] = ext_ref[tm + first:tm + CONV_HALO, :]
    ext_ref[0:CONV_HALO, :] = ext_ref[tm:tm + CONV_HALO, :]
    a_out = _silu(_layernorm(a_conv, clg_ref[...], clb_ref[...]))

    row = lax.broadcasted_iota(jnp.int32, (CHUNK, CHUNK), 0)
    col = lax.broadcasted_iota(jnp.int32, (CHUNK, CHUNK), 1)
    wm = [jnp.where(col <= row, sw_ref[g], 0.0).astype(BF16) for g in range(B_GROUPS)]
    vb = v.astype(BF16)
    gate_rows = []
    for c in range(tm // CHUNK):
        cols = []
        for g in range(B_GROUPS):
            vg = vb[c * CHUNK:(c + 1) * CHUNK, g * B_GROUP_DIM:(g + 1) * B_GROUP_DIM]
            cols.append(_dot(wm[g], vg) + sbt_ref[:, g:g + 1])
        gate_rows.append(jnp.concatenate(cols, axis=1))
    b_out = u * jnp.concatenate(gate_rows, axis=0)

    y_ref[0, :, :D_A] = a_out.astype(BF16)
    y_ref[0, :, D_A:] = b_out.astype(BF16)
    chv_ref[0] = v[tm - CHUNK:, :]


def _even_prompt(x, mods, g_mix, w_in, conv_w, conv_b, cln_g, cln_b, sln_g, sln_b, sgu_w, sgu_bt, *, tm):
    bn, ln, _ = x.shape
    assert ln % tm == 0 and tm % CHUNK == 0 and ln >= CONV_HALO
    return pl.pallas_call(
        functools.partial(_even_prompt_body, tm=tm, conv_rows=64),
        out_shape=(
            jax.ShapeDtypeStruct((bn, ln, D_MODEL), BF16),
            jax.ShapeDtypeStruct((bn, CONV_WIDTH - 1, D_A), F32),
            jax.ShapeDtypeStruct((bn, CHUNK, D_B), F32),
        ),
        grid=(bn, ln // tm),
        in_specs=[
            pl.BlockSpec((1, tm, D_MODEL), lambda b, t: (b, t, 0)),
            pl.BlockSpec((6, 1, 1, D_MODEL), lambda b, t: (0, b, 0, 0)),
            _const_spec((1, D_MODEL)),
            _const_spec((D_MODEL, 2 * D_A + 2 * D_B)),
            _const_spec((CONV_WIDTH, D_A)),
            _const_spec((1, D_A)), _const_spec((1, D_A)), _const_spec((1, D_A)),
            _const_spec((1, D_B)), _const_spec((1, D_B)),
            _const_spec((B_GROUPS, CHUNK, CHUNK)),
            _const_spec((CHUNK, B_GROUPS)),
        ],
        out_specs=(
            pl.BlockSpec((1, tm, D_MODEL), lambda b, t: (b, t, 0)),
            pl.BlockSpec((1, CONV_WIDTH - 1, D_A), lambda b, t: (b, 0, 0)),
            pl.BlockSpec((1, CHUNK, D_B), lambda b, t: (b, 0, 0)),
        ),
        scratch_shapes=[pltpu.VMEM((tm + CONV_HALO, D_A), F32),
                        pltpu.VMEM((SUBLANES - 1, tm + CONV_HALO - SUBLANES, D_A), F32)],
        compiler_params=_cparams("arbitrary", "arbitrary"),
        name="even_prompt",
    )(x, mods, g_mix, w_in, conv_w, conv_b, cln_g, cln_b, sln_g, sln_b, sgu_w, sgu_bt)


def _even_sample_body(x_ref, mod_ref, gmix_ref, win_ref, st_ref, cw_ref, cb_ref, clg_ref, clb_ref, slg_ref,
                      slb_ref, w00_ref, b0_ref, y_ref, a_ref, v_ref):
    h = _modulate(x_ref[...], gmix_ref[...], mod_ref[0], mod_ref[1]).astype(BF16)
    z = _dot(h, win_ref[...])
    a = z[:, :D_A] * _sigmoid(z[:, D_A:2 * D_A])
    u = _gelu(z[:, 2 * D_A:2 * D_A + D_B])
    v = _layernorm(_gelu(z[:, 2 * D_A + D_B:]), slg_ref[...], slb_ref[...])
    acc = a * cw_ref[CONV_WIDTH - 1:CONV_WIDTH, :] + cb_ref[...]
    for w in range(CONV_WIDTH - 1):
        acc = acc + st_ref[w] * cw_ref[w:w + 1, :]
    a_out = _silu(_layernorm(acc, clg_ref[...], clb_ref[...]))
    b_out = u * (w00_ref[...] * v + b0_ref[...])
    y_ref[:, :D_A] = a_out.astype(BF16)
    y_ref[:, D_A:] = b_out.astype(BF16)
    a_ref[...] = a
    v_ref[...] = v


def _even_sample(x, mods, g_mix, w_in, state_t, conv_w, conv_b, cln_g, cln_b, sln_g, sln_b, w00, b0):
    sn = x.shape[0]
    shapes = [a.shape for a in (x, mods, g_mix, w_in, state_t, conv_w, conv_b, cln_g, cln_b, sln_g, sln_b, w00, b0)]
    return pl.pallas_call(
        _even_sample_body,
        out_shape=(
            jax.ShapeDtypeStruct((sn, D_MODEL), BF16),
            jax.ShapeDtypeStruct((sn, D_A), F32),
            jax.ShapeDtypeStruct((sn, D_B), F32),
        ),
        grid=(1,),
        in_specs=[_const_spec(s) for s in shapes],
        out_specs=(_const_spec((sn, D_MODEL)), _const_spec((sn, D_A)), _const_spec((sn, D_B))),
        compiler_params=_cparams("arbitrary"),
        name="even_sample",
    )(x, mods, g_mix, w_in, state_t, conv_w, conv_b, cln_g, cln_b, sln_g, sln_b, w00, b0)


def _proj_body(x_ref, mod_ref, gmix_ref, w_ref, cos_ref, sin_ref,
               qs_ref, qrs_ref, rowc_ref, rows_ref, rowsb_ref, cols_ref, colsb_ref, g_ref):
    tm = x_ref.shape[1]
    h = _modulate(x_ref[0], gmix_ref[...], mod_ref[0, 0], mod_ref[1, 0]).astype(BF16)
    z = _dot(h, w_ref[...])
    cos = cos_ref[...]
    sin = sin_ref[...]
    q0 = N_HEADS * HEAD_DIM
    low = _lane_iota((tm, LANES)) < HEAD_DIM
    for c in range(q0 // LANES):
        xs = z[:, c * LANES:(c + 1) * LANES] * SCALE
        xr = _rope_pair(xs, cos, sin)
        for half in range(2):
            hd = 2 * c + half
            par = (hd // GROUP) % 2
            keep = low if par == 0 else jnp.logical_not(low)
            for src, ref in ((xs, qs_ref), (xr, qrs_ref)):
                v = src if half == par else pltpu.roll(src, HEAD_DIM, 1)
                ref[0, hd] = jnp.where(keep, v, 0.0).astype(BF16)

    def kv(i):
        return z[:, q0 + i * KV_DIM:q0 + (i + 1) * KV_DIM]

    def kv_rot(i):
        base = q0 + i * KV_DIM
        return jnp.concatenate(
            [_rope_pair(z[:, base + c * LANES:base + (c + 1) * LANES], cos, sin) for c in range(N_PAIR)], axis=1)

    rows = jnp.concatenate([kv(0), kv(1), kv_rot(2), kv(3), kv_rot(4), kv(5)], axis=1)
    rowc_ref[0] = rows[:, :2 * KV_DIM]
    rows_ref[0] = rows[:, 2 * KV_DIM:]
    rowsb_ref[0] = rows[:, 2 * KV_DIM:].astype(BF16)
    cols = rows.T
    cols_ref[0] = cols
    colsb_ref[0] = cols[2 * KV_DIM:].astype(BF16)
    g_ref[0] = _sigmoid(z[:, q0 + 6 * KV_DIM:])


def _nsa_project(x, mods, g_mix, w_in, cos, sin, *, tm):
    bn, ln, _ = x.shape
    r = mods.shape[2]
    tm = min(tm, ln)
    mod_rows = 1 if r == 1 else tm
    mod_map = (lambda b, t: (0, b, 0, 0)) if r == 1 else (lambda b, t: (0, b, t, 0))
    tok = lambda w: pl.BlockSpec((1, tm, w), lambda b, t: (b, t, 0))
    col = lambda w: pl.BlockSpec((1, w, tm), lambda b, t: (b, 0, t))
    head = pl.BlockSpec((1, N_HEADS, tm, LANES), lambda b, t: (b, 0, t, 0))
    return pl.pallas_call(
        _proj_body,
        out_shape=(
            jax.ShapeDtypeStruct((bn, N_HEADS, ln, LANES), BF16),
            jax.ShapeDtypeStruct((bn, N_HEADS, ln, LANES), BF16),
            jax.ShapeDtypeStruct((bn, ln, 2 * KV_DIM), F32),
            jax.ShapeDtypeStruct((bn, ln, 4 * KV_DIM), F32),
            jax.ShapeDtypeStruct((bn, ln, 4 * KV_DIM), BF16),
            jax.ShapeDtypeStruct((bn, 6 * KV_DIM, ln), F32),
            jax.ShapeDtypeStruct((bn, 4 * KV_DIM, ln), BF16),
            jax.ShapeDtypeStruct((bn, ln, LANES), F32),
        ),
        grid=(bn, ln // tm),
        in_specs=[
            tok(D_MODEL),
            pl.BlockSpec((6, 1, mod_rows, D_MODEL), mod_map),
            _const_spec((1, D_MODEL)),
            _const_spec((D_MODEL, ODD_IN_PAD)),
            pl.BlockSpec((tm, LANES), lambda b, t: (t, 0)),
            pl.BlockSpec((tm, LANES), lambda b, t: (t, 0)),
        ],
        out_specs=(head, head, tok(2 * KV_DIM), tok(4 * KV_DIM), tok(4 * KV_DIM), col(6 * KV_DIM), col(4 * KV_DIM),
                   tok(LANES)),
        compiler_params=_cparams("arbitrary", "arbitrary"),
        name="nsa_project",
    )(x, mods, g_mix, w_in, cos, sin)


def _compress_rows(load_rows, pe2_ref, w1_ref, w2e_ref, w2o_ref):
    n_chunk = 128
    lane = _lane_iota((n_chunk, LANES))
    low = lane < HEAD_DIM
    heads = [[] for _ in range(N_KV)]
    for p in range(CMP_STRIDE // 2):
        for m in range(N_PAIR):
            lo = load_rows(2 * p, m)
            hi = load_rows(2 * p + 1, m)
            heads[2 * m].append(jnp.where(low, lo, pltpu.roll(hi, HEAD_DIM, 1)))
            heads[2 * m + 1].append(jnp.where(low, pltpu.roll(lo, HEAD_DIM, 1), hi))
    x_all = jnp.concatenate([jnp.concatenate(hp, axis=1) for hp in heads], axis=0).astype(BF16)
    ab = _dot(x_all, w1_ref[...])
    pe = _dot(pe2_ref[...], w1_ref[...])
    hid0 = pe[0:1, :CMP_HIDDEN] + pe[1:2, CMP_HIDDEN:]
    hid = []
    for k in range(N_KV):
        a_part = ab[k * n_chunk:(k + 1) * n_chunk, :CMP_HIDDEN]
        b_part = ab[k * n_chunk:(k + 1) * n_chunk, CMP_HIDDEN:]
        nxt = pltpu.roll(b_part, n_chunk - 1, 0)
        hid.append(_gelu(a_part + nxt + hid0).astype(BF16))
    out = []
    for m in range(N_PAIR):
        out.append(_dot(hid[2 * m], w2e_ref[...]) + _dot(hid[2 * m + 1], w2o_ref[...]))
    return jnp.concatenate(out, axis=1)


def _compress_prompt_body(rowc_ref, pek_ref, w1k_ref, w2ke_ref, w2ko_ref, pev_ref, w1v_ref, w2ve_ref, w2vo_ref,
                          kc_ref, vc_ref):
    slabs = 2 * N_PAIR
    n_chunk = rowc_ref.shape[1] // (slabs * CMP_STRIDE)
    assert n_chunk == 128

    def rows(first):
        return lambda j, m: rowc_ref[0, pl.ds(slabs * j + first + m, n_chunk, stride=slabs * CMP_STRIDE), :]

    kc_ref[0] = _compress_rows(rows(0), pek_ref, w1k_ref, w2ke_ref, w2ko_ref).astype(BF16)
    vc_ref[0] = _compress_rows(rows(N_PAIR), pev_ref, w1v_ref, w2ve_ref, w2vo_ref).astype(BF16)


def _compress_prompt(rowc, cw_k, cw_v):
    bn, ln, w = rowc.shape
    rowc = rowc.reshape(bn, ln * w // LANES, LANES)
    wspecs = [_const_spec(a.shape) for a in cw_k + cw_v]
    out = pl.BlockSpec((1, 128, KV_DIM), lambda b: (b, 0, 0))
    return pl.pallas_call(
        _compress_prompt_body,
        out_shape=(jax.ShapeDtypeStruct((bn, 128, KV_DIM), BF16),) * 2,
        grid=(bn,),
        in_specs=[pl.BlockSpec((1,) + rowc.shape[1:], lambda b: (b, 0, 0))] + wspecs,
        out_specs=(out, out),
        compiler_params=_cparams("arbitrary"),
        name="compress_prompt",
    )(rowc, *cw_k, *cw_v)


ONES_ROWS = 16


def _flash_step(carry, s, v1):
    m_i, acc = carry
    m_new = jnp.maximum(m_i, jnp.max(s, 0, keepdims=True))
    alpha = jnp.exp(m_i - m_new)
    p = jnp.exp(s - m_new)
    return m_new, alpha * acc + _dot(v1, p.astype(BF16))


def _attn_prompt_body(qs_ref, qrs_ref, g_ref, kc_ref, vc_ref, krow_ref, kvt_ref, ovt_ref, nege_ref, o_ref, *, t):
    qi = pl.program_id(1)
    q0 = pl.multiple_of(qi * t, t)
    rows4 = GROUP * t
    pos = q0 + lax.broadcasted_iota(jnp.int32, (t, 1), 0)
    pos_t = q0 + (_lane_iota((1, rows4)) & (t - 1))
    gates_t = g_ref[0].T

    def q_t(ref, k):
        return ref[0, GROUP * k:GROUP * (k + 1)].reshape(rows4, LANES).astype(F32).T.astype(BF16)

    def per_token(x):
        return jnp.concatenate([x] * GROUP, axis=1)

    key_n = lax.broadcasted_iota(jnp.int32, (LANES, 1), 0)
    cmask = (key_n < LANES - 1) & (key_n * CMP_STRIDE + (CMP_BLOCK - 1) <= pos_t)
    o_cmp = []
    imp_t = jnp.zeros((LANES, t), F32)
    for k in range(N_KV):
        m = k // 2
        s = jnp.where(cmask, _dot(kc_ref[0, :, m * LANES:(m + 1) * LANES], q_t(qs_ref, k)), NEG)
        p = jnp.where(cmask, jnp.exp(s - jnp.max(s, 0, keepdims=True)), 0.0)
        p = p / jnp.maximum(jnp.sum(p, 0, keepdims=True), 1e-30)
        o_cmp.append(_dot(vc_ref[0, :, m * LANES:(m + 1) * LANES].astype(F32).T.astype(BF16), p.astype(BF16)))
        psum = p[:, 0:t] + p[:, t:2 * t] + p[:, 2 * t:3 * t] + p[:, 3 * t:4 * t]
        imp_t = imp_t + jnp.dot(ovt_ref[k], psum, preferred_element_type=F32, precision=HIGHEST)
    unsel = 1.0 - _select_blocks(imp_t.T, pos, N_SEL)
    unsel_t = per_token(unsel.T.astype(BF16))
    ones = jnp.ones((ONES_ROWS, t), BF16)

    kr = lax.broadcasted_iota(jnp.int32, (t, t), 0)
    qc = lax.broadcasted_iota(jnp.int32, (t, t), 1)
    bias_diag = per_token(jnp.where(kr <= qc, 0.0, MASK_BIAS))
    bias_edge = per_token(jnp.where(kr >= qc, 0.0, MASK_BIAS))

    def k_tile(stream, m, kt):
        k0 = pl.multiple_of(kt * t, t)
        base = stream * KV_DIM + m * LANES
        return krow_ref[0, pl.ds(k0, t), base:base + LANES]

    def v_ones(stream, m, kt):
        k0 = pl.multiple_of(kt * t, t)
        base = stream * KV_DIM + m * LANES
        return jnp.concatenate([kvt_ref[0, base:base + LANES, pl.ds(k0, t)], ones], axis=0)

    def init():
        return jnp.full((1, rows4), NEG, F32), jnp.zeros((LANES + ONES_ROWS, rows4), F32)

    def finish(carry):
        _, acc = carry
        return acc[:LANES] / jnp.maximum(acc[LANES:LANES + 1], 1e-30)

    heads = [None] * N_HEADS
    for k in range(N_KV):
        m = k // 2
        dims = slice((k % 2) * HEAD_DIM, (k % 2 + 1) * HEAD_DIM)
        qr_t = q_t(qrs_ref, k)

        qu_t = jnp.concatenate([qr_t, unsel_t], axis=0)

        def sel_scores(kt, m=m, k=k, qu_t=qu_t):
            k0 = pl.multiple_of(kt * t, t)
            return _dot(jnp.concatenate([k_tile(0, m, kt), nege_ref[k, pl.ds(k0, t), :]], axis=1), qu_t)

        def sel_step(kt, cr, m=m, sel_scores=sel_scores):
            return _flash_step(cr, sel_scores(kt), v_ones(1, m, kt))

        carry = lax.fori_loop(0, qi, sel_step, init())
        carry = _flash_step(carry, sel_scores(qi) + bias_diag, v_ones(1, m, qi))
        o_sel = finish(carry)

        carry = init()
        for back, bias in ((2, bias_edge), (1, None)):
            def visit(cr, back=back, bias=bias, m=m, qr_t=qr_t):
                s = _dot(k_tile(2, m, qi - back), qr_t)
                return _flash_step(cr, s if bias is None else s + bias, v_ones(3, m, qi - back))
            carry = lax.cond(qi >= back, visit, lambda cr: cr, carry)
        carry = _flash_step(carry, _dot(k_tile(2, m, qi), qr_t) + bias_diag, v_ones(3, m, qi))
        o_win = finish(carry)

        for g in range(GROUP):
            h = GROUP * k + g
            cols = slice(g * t, (g + 1) * t)
            heads[h] = (gates_t[3 * h:3 * h + 1] * o_cmp[k][dims, cols] + gates_t[3 * h + 1:3 * h + 2] * o_sel[dims, cols]
                        + gates_t[3 * h + 2:3 * h + 3] * o_win[dims, cols])
    o_ref[0] = jnp.concatenate(heads, axis=0).T.astype(BF16)


def _attn_prompt(qs, qrs, gates, kc, vc, krow, kvt, ov_t, nege, *, t):
    bn, _, ln, _ = qs.shape
    assert ln % t == 0 and ln <= SEL_LANES * SEL_BLOCK and WINDOW == 2 * t and t & (t - 1) == 0
    head = pl.BlockSpec((1, N_HEADS, t, LANES), lambda b, i: (b, 0, i, 0))
    cmp_spec = pl.BlockSpec((1, 128, KV_DIM), lambda b, i: (b, 0, 0))
    return pl.pallas_call(
        functools.partial(_attn_prompt_body, t=t),
        out_shape=jax.ShapeDtypeStruct((bn, ln, D_MODEL), BF16),
        grid=(bn, ln // t),
        in_specs=[head, head, pl.BlockSpec((1, t, LANES), lambda b, i: (b, i, 0)), cmp_spec, cmp_spec,
                  pl.BlockSpec((1, ln, 4 * KV_DIM), lambda b, i: (b, 0, 0)),
                  pl.BlockSpec((1, 4 * KV_DIM, ln), lambda b, i: (b, 0, 0)),
                  _const_spec(ov_t.shape), _const_spec(nege.shape)],
        out_specs=pl.BlockSpec((1, t, D_MODEL), lambda b, i: (b, i, 0)),
        compiler_params=_cparams("arbitrary", "arbitrary"),
        name="attn_prompt",
    )(qs, qrs, gates, kc, vc, krow, kvt, ov_t, nege)


def _attn_sample_body(pt_ref, qs_ref, qrs_ref, g_ref, new_ref, wkc_ref, wvc_ref, wks_ref, wvs_ref, *rest, n_pages):
    del pt_ref
    pools = [rest[i * n_pages:(i + 1) * n_pages] for i in range(4)]
    (pek_ref, w1k_ref, w2ke_ref, w2ko_ref, pev_ref, w1v_ref, w2ve_ref, w2vo_ref, ov_ref, expand_ref,
     o_ref, wko_ref, wvo_ref, x_ref) = rest[4 * n_pages:]
    past = n_pages * PAGE_SIZE

    def compress(pages, *weights):
        for i, pg in enumerate(pages):
            for m in range(N_PAIR):
                x_ref[m, i * PAGE_SIZE:(i + 1) * PAGE_SIZE, :] = pg[0, 0, m * LANES:(m + 1) * LANES, :].T
        load = lambda j, m: x_ref[m, pl.ds(j, past // CMP_STRIDE, stride=CMP_STRIDE), :]
        return _compress_rows(load, *weights).astype(BF16)

    kc = compress(pools[0], pek_ref, w1k_ref, w2ke_ref, w2ko_ref)
    vc = compress(pools[1], pev_ref, w1v_ref, w2ve_ref, w2vo_ref)

    lane = _lane_iota((SUBLANES, LANES))
    upper_rows = lax.broadcasted_iota(jnp.int32, (SUBLANES, LANES), 0) >= GROUP
    gates = g_ref[0]
    pos = jnp.full((SUBLANES, 1), past, jnp.int32)

    cmask = lane < LANES - 1
    o_cmp = []
    imp = jnp.zeros((SUBLANES, LANES), F32)
    for m in range(N_PAIR):
        qs = qs_ref[0, 2 * GROUP * m:2 * GROUP * (m + 1), :]
        p = _masked_softmax(_dot_nt(qs, kc[:, m * LANES:(m + 1) * LANES]), cmask)
        o_cmp.append(_dot(p.astype(BF16), vc[:, m * LANES:(m + 1) * LANES]))
        for kk in range(2):
            psum = jnp.sum(p[kk * GROUP:(kk + 1) * GROUP], axis=0, keepdims=True)
            imp = imp + jnp.dot(jnp.broadcast_to(psum, (SUBLANES, LANES)), ov_ref[2 * m + kk],
                                preferred_element_type=F32, precision=HIGHEST)
    sel = _select_blocks(imp, pos, N_SEL - 1)
    seg = lane >> SEL_LANES_SHIFT
    expand = expand_ref[...]

    def new_row(stream, m):
        base = stream * KV_DIM + m * LANES
        return new_ref[0, :, base:base + LANES]

    for m in range(N_PAIR):
        rows = slice(m * LANES, (m + 1) * LANES)
        qrs = qrs_ref[0, 2 * GROUP * m:2 * GROUP * (m + 1), :]
        qrf = qrs.astype(F32)
        want = jnp.where(upper_rows, 2 * m + 1, 2 * m)
        sel_rows = jnp.where(seg == want, sel, 0.0).astype(BF16)
        ok = _dot(sel_rows, expand) > 0.5
        s = jnp.concatenate([_dot(qrs, pg[0, 0, rows, :].astype(BF16)) for pg in pools[2]], axis=1)
        s_new = jnp.sum(qrf * new_row(0, m), -1, keepdims=True)
        s = jnp.where(ok, s, NEG)
        mx = jnp.maximum(jnp.max(s, -1, keepdims=True), s_new)
        p = jnp.where(ok, jnp.exp(s - mx), 0.0)
        p_new = jnp.exp(s_new - mx)
        denom = jnp.maximum(jnp.sum(p, -1, keepdims=True) + p_new, 1e-30)
        pb = p.astype(BF16)
        acc = p_new * new_row(1, m)
        for i, pg in enumerate(pools[3]):
            acc = acc + _dot_nt(pb[:, i * PAGE_SIZE:(i + 1) * PAGE_SIZE], pg[0, 0, rows, :].astype(BF16))
        o_sel = acc / denom
        sw = _dot(qrs, wks_ref[0, 0, rows, :].astype(BF16))
        sw_new = jnp.sum(qrf * new_row(2, m), -1, keepdims=True)
        mw = jnp.maximum(jnp.max(sw, -1, keepdims=True), sw_new)
        pw = jnp.exp(sw - mw)
        pw_new = jnp.exp(sw_new - mw)
        dw = jnp.maximum(jnp.sum(pw, -1, keepdims=True) + pw_new, 1e-30)
        o_win = (_dot_nt(pw.astype(BF16), wvs_ref[0, 0, rows, :].astype(BF16)) + pw_new * new_row(3, m)) / dw
        for kk in range(2):
            for c in range(GROUP // 2):
                halves = []
                for gg in range(2):
                    g = 2 * c + gg
                    h = GROUP * (2 * m + kk) + g
                    r = kk * GROUP + g
                    o = (gates[:, 3 * h:3 * h + 1] * o_cmp[m][r:r + 1] + gates[:, 3 * h + 1:3 * h + 2] * o_sel[r:r + 1]
                         + gates[:, 3 * h + 2:3 * h + 3] * o_win[r:r + 1])
                    if gg != kk:
                        o = pltpu.roll(jnp.broadcast_to(o, (SUBLANES, LANES)), HEAD_DIM, 1)[0:1]
                    halves.append(o)
                col = (GROUP * (2 * m + kk)) // 2 + c
                o_ref[0, :, col * LANES:(col + 1) * LANES] = jnp.where(
                    _lane_iota((1, LANES)) < HEAD_DIM, halves[0], halves[1]).astype(BF16)

    nbuf = wks_ref.shape[-1]
    last = _lane_iota((KV_DIM, nbuf)) == nbuf - 1
    wko_ref[0] = jnp.where(last, wkc_ref[0], pltpu.roll(wks_ref[0, 0], nbuf - 1, 1))
    wvo_ref[0] = jnp.where(last, wvc_ref[0], pltpu.roll(wvs_ref[0, 0], nbuf - 1, 1))


def _attn_sample(page_table, layer, qs, qrs, gates, new_rows, wk_col, wv_col, win_k, win_v,
                 pool_ck, pool_cv, pool_sk, pool_sv, cw_k, cw_v, ov):
    sn = qs.shape[0]
    n_pages = page_table.shape[1]
    nbuf = win_k.shape[-1]
    past = n_pages * PAGE_SIZE
    assert nbuf == WINDOW and past == SEL_LANES * SEL_BLOCK
    per_seq = lambda *blk: pl.BlockSpec((1,) + blk, lambda s, pt: (s,) + (0,) * len(blk))
    page_map = lambda s, pt, p: (layer, pt[s, p], 0, 0)
    page_specs = [pl.BlockSpec((1, 1, KV_DIM, PAGE_SIZE), functools.partial(page_map, p=p)) for p in range(n_pages)]
    win_spec = pl.BlockSpec((1, 1, KV_DIM, nbuf), lambda s, pt: (layer, s, 0, 0))
    expand = (np.arange(LANES)[:, None] % SEL_LANES) == (np.arange(past)[None, :] // SEL_BLOCK)
    weights = cw_k + cw_v + (ov, jnp.asarray(expand, dtype=BF16))
    grid_spec = pltpu.PrefetchScalarGridSpec(
        num_scalar_prefetch=1,
        grid=(sn,),
        in_specs=[per_seq(N_HEADS, LANES), per_seq(N_HEADS, LANES), per_seq(1, LANES), per_seq(1, 4 * KV_DIM),
                  per_seq(KV_DIM, 1), per_seq(KV_DIM, 1), win_spec, win_spec]
        + page_specs * 4
        + [pl.BlockSpec(w.shape, functools.partial(lambda s, pt, n: (0,) * n, n=w.ndim)) for w in weights],
        out_specs=(per_seq(1, D_MODEL), per_seq(KV_DIM, nbuf), per_seq(KV_DIM, nbuf)),
        scratch_shapes=[pltpu.VMEM((N_PAIR, past, LANES), F32)],
    )
    pools = [pool_ck] * n_pages + [pool_cv] * n_pages + [pool_sk] * n_pages + [pool_sv] * n_pages
    return pl.pallas_call(
        functools.partial(_attn_sample_body, n_pages=n_pages),
        out_shape=(jax.ShapeDtypeStruct((sn, 1, D_MODEL), BF16),
                   jax.ShapeDtypeStruct((sn, KV_DIM, nbuf), F32),
                   jax.ShapeDtypeStruct((sn, KV_DIM, nbuf), F32)),
        grid_spec=grid_spec,
        compiler_params=_cparams("arbitrary"),
        name="attn_sample",
    )(page_table, qs, qrs, gates, new_rows, wk_col, wv_col, win_k, win_v, *pools, *weights)


def _rope_tables(pos):
    half = HEAD_DIM // 2
    inv = jnp.power(jnp.float32(ROPE_THETA), -jnp.arange(half, dtype=F32) * (2.0 / HEAD_DIM))
    ang = pos.astype(F32)[:, None] * inv[None, :]
    cos, sin = jnp.cos(ang), jnp.sin(ang)
    return jnp.concatenate([cos, cos, cos, cos], -1), jnp.concatenate([-sin, sin, -sin, sin], -1)


def _overlap_tables():
    n_cmp, n_sel = LANES - 1, SEL_LANES
    i = np.arange(n_cmp)[:, None] * CMP_STRIDE
    j = np.arange(n_sel)[None, :] * SEL_BLOCK
    ov = ((i < j + SEL_BLOCK) & (i + CMP_BLOCK > j)).astype(np.float32)
    out = np.zeros((N_KV, LANES, LANES), np.float32)
    for k in range(N_KV):
        out[k, :n_cmp, k * SEL_LANES:(k + 1) * SEL_LANES] = ov
    return jnp.asarray(out)


def _unselected_bias_table(ln):
    c = np.arange(ln)[:, None]
    r = np.arange(LANES)[None, :]
    out = np.zeros((N_KV, ln, LANES), np.float32)
    for k in range(N_KV):
        out[k] = np.where((r // SEL_LANES == k) & (r % SEL_LANES == c // SEL_BLOCK), MASK_BIAS, 0.0)
    return jnp.asarray(out, dtype=BF16)


def _rows_last(t):
    nd = t.ndim
    t = jnp.transpose(t, tuple(range(nd - 3)) + (nd - 2, nd - 1, nd - 3))
    return t.reshape(t.shape[:-3] + (KV_DIM, t.shape[-1]))


def _rows_first(t):
    nd = t.ndim
    t = t.reshape(t.shape[:-2] + (N_KV, HEAD_DIM, t.shape[-1]))
    return jnp.transpose(t, tuple(range(nd - 2)) + (nd, nd - 2, nd - 1))


def _compress_weights(pe, w1, w2):
    r = CMP_BLOCK // CMP_STRIDE
    flat = CMP_STRIDE * HEAD_DIM
    w1cat = jnp.concatenate([w1[m * CMP_STRIDE:(m + 1) * CMP_STRIDE].reshape(flat, CMP_HIDDEN) for m in range(r)], 1)
    pe2 = jnp.zeros((SUBLANES, flat), F32).at[:r].set(pe.reshape(r, flat))
    zeros = jnp.zeros_like(w2)
    return (pe2.astype(BF16), w1cat.astype(BF16), jnp.concatenate([w2, zeros], 1).astype(BF16),
            jnp.concatenate([zeros, w2], 1).astype(BF16))


def kernel(x_prompt, x_sample, state_conv, cache_cmp_k, cache_cmp_v, cache_sel_k, cache_sel_v, state_win_k, state_win_v, page_table, c_prompt, c_sample, ada_w, ada_b, norm_mix_g, norm_ffn_g, ffn_w1, ffn_w2, even_w_in, even_w_out, conv_w, conv_b, conv_ln_g, conv_ln_b, sgu_ln_g, sgu_ln_b, sgu_w, sgu_b, odd_w_in, odd_w_out, cmp_pe_k, cmp_w1_k, cmp_w2_k, cmp_pe_v, cmp_w1_v, cmp_w2_v, final_norm_g):
    depth = ada_w.shape[0]
    bn, ln, _ = x_prompt.shape
    sn = x_sample.shape[0]
    past = page_table.shape[1] * PAGE_SIZE

    ada = _ada_all(jnp.concatenate([c_prompt, c_sample], 0), ada_w, ada_b)
    ov = _overlap_tables()
    ov_t = jnp.swapaxes(ov, 1, 2)
    nege = _unselected_bias_table(ln)
    cos_p, sin_p = _rope_tables(jnp.arange(ln))
    cos_s, sin_s = _rope_tables(jnp.full((sn,), past))
    g_fin = final_norm_g.reshape(1, D_MODEL)
    row = lambda a: a.reshape(1, -1)
    pools = [_rows_last(c) for c in (cache_cmp_k, cache_cmp_v, cache_sel_k, cache_sel_v)]
    win_k_all, win_v_all = _rows_last(state_win_k), _rows_last(state_win_v)

    xp = x_prompt
    xs = x_sample.reshape(1, sn, D_MODEL)
    conv_p, conv_s, chv_p, chv_s = [], [], [], []
    cols_p, cols_s, win_s = [], [], []
    for l in range(depth):
        mods_p = ada[l, :, :bn].reshape(6, bn, 1, D_MODEL)
        mods_s = ada[l, :, bn:].reshape(6, 1, sn, D_MODEL)
        g_mix = row(norm_mix_g[l])
        if l % 2 == 0:
            e = l // 2
            w_in = even_w_in[e].astype(BF16)
            w_out = even_w_out[e].astype(BF16)
            cln = (row(conv_b[e]), row(conv_ln_g[e]), row(conv_ln_b[e]), row(sgu_ln_g[e]), row(sgu_ln_b[e]))
            yp, bp, vp = _even_prompt(xp, mods_p, g_mix, w_in, conv_w[e], *cln, sgu_w[e], sgu_b[e].T, tm=256)
            w00 = row(jnp.repeat(sgu_w[e, :, 0, 0], B_GROUP_DIM))
            b0 = row(jnp.repeat(sgu_b[e, :, 0], B_GROUP_DIM))
            ys, a_s, v_s = _even_sample(xs[0], mods_s[:, 0], g_mix, w_in, jnp.swapaxes(state_conv[e], 0, 1),
                                        conv_w[e], *cln, w00, b0)
            conv_p.append(bp)
            conv_s.append(jnp.concatenate([state_conv[e][:, 1:], a_s[:, None, :]], axis=1))
            chv_p.append(vp)
            chv_s.append(v_s[:, None, :])
            ys = ys[None]
        else:
            o = l // 2
            w_in = jnp.pad(odd_w_in[o], ((0, 0), (0, ODD_IN_PAD - odd_w_in.shape[-1]))).astype(BF16)
            w_out = odd_w_out[o].astype(BF16)
            cw_k = _compress_weights(cmp_pe_k[o], cmp_w1_k[o], cmp_w2_k[o])
            cw_v = _compress_weights(cmp_pe_v[o], cmp_w1_v[o], cmp_w2_v[o])
            qs, qrs, rowc, _, rowsb, cols, colsb, gates = _nsa_project(xp, mods_p, g_mix, w_in, cos_p, sin_p, tm=256)
            kc, vc = _compress_prompt(rowc, cw_k, cw_v)
            yp = _attn_prompt(qs, qrs, gates, kc, vc, rowsb, colsb, ov_t, nege, t=256)
            cols_p.append(cols)
            qs, qrs, _, rows, _, cols, _, gates = _nsa_project(xs, mods_s, g_mix, w_in, cos_s, sin_s, tm=sn)
            heads = lambda t: jnp.swapaxes(t[0], 0, 1)
            col = lambda i: rows[0, :, i * KV_DIM:(i + 1) * KV_DIM].reshape(sn, KV_DIM, 1)
            ys, wk_new, wv_new = _attn_sample(
                page_table, o, heads(qs), heads(qrs), gates.reshape(sn, 1, LANES), rows.reshape(sn, 1, 4 * KV_DIM),
                col(2), col(3), win_k_all, win_v_all, *pools, cw_k, cw_v, ov)
            ys = ys.reshape(1, sn, D_MODEL)
            cols_s.append(cols[0])
            win_s.append((wk_new, wv_new))
        final = l == depth - 1
        w1 = ffn_w1[l].astype(BF16)
        w2 = ffn_w2[l].astype(BF16)
        g_ffn = row(norm_ffn_g[l])
        xp = _post(xp, yp, mods_p, g_ffn, w_out, w1, w2, g_fin, final=final, tm=512)
        xs = _post(xs, ys, mods_s, g_ffn, w_out, w1, w2, g_fin, final=final, tm=sn)
    st = lambda lst: jnp.stack(lst, axis=0)
    nw = min(WINDOW, ln)
    cols_p = st(cols_p)
    stream_p = lambda i, lo: _rows_first(cols_p[:, :, i * KV_DIM:(i + 1) * KV_DIM, lo:])
    cols_s = st(cols_s)
    stream_s = lambda i: _rows_first(cols_s[:, i * KV_DIM:(i + 1) * KV_DIM, :])[:, :, None]
    return (xp, xs.reshape(sn, 1, D_MODEL),
            st(conv_p), st(conv_s), st(chv_p), st(chv_s),
            stream_p(0, 0), stream_p(1, 0), stream_p(2, 0), stream_p(3, 0), stream_p(4, ln - nw), stream_p(5, ln - nw),
            stream_s(0), stream_s(1), stream_s(2), stream_s(3),
            _rows_first(st([w[0] for w in win_s])), _rows_first(st([w[1] for w in win_s])))
```

```python
import functools

import numpy as np
import jax
import jax.numpy as jnp
from jax import lax
from jax.experimental import pallas as pl
from jax.experimental.pallas import tpu as pltpu

F32 = jnp.float32
BF16 = jnp.bfloat16
HIGHEST = lax.Precision.HIGHEST

LANES = 128
SUBLANES = 8
VMEM_LIMIT_BYTES = 56 * 1024 * 1024

D_MODEL = 1024
D_FF = 4 * D_MODEL
D_A = D_MODEL // 2
CONV_WIDTH = 31
CONV_HALO = 32
D_B = D_MODEL // 2
B_GROUPS = 4
B_GROUP_DIM = D_B // B_GROUPS
CHUNK = 128
N_HEADS = 16
HEAD_DIM = 64
N_KV = 4
GROUP = N_HEADS // N_KV
KV_DIM = N_KV * HEAD_DIM
N_PAIR = KV_DIM // LANES
CMP_BLOCK = 32
CMP_STRIDE = 16
CMP_HIDDEN = 2 * HEAD_DIM
SEL_BLOCK = 64
N_SEL = 8
SEL_LANES = 32
SEL_SHIFT = 6
SEL_LANES_SHIFT = 5
WINDOW = 512
PAGE_SIZE = 128
ROPE_THETA = 10000.0
EPS = 1e-6
NEG = -1e30
MASK_BIAS = -(2.0 ** 100)
FORCED = 1e4
SCALE = HEAD_DIM ** -0.5
ODD_IN_PAD = N_HEADS * HEAD_DIM + 6 * KV_DIM + LANES


def _cparams(*sem):
    return pltpu.CompilerParams(dimension_semantics=sem, vmem_limit_bytes=VMEM_LIMIT_BYTES)


def _const_spec(shape):
    n = len(shape)
    return pl.BlockSpec(shape, lambda *_: (0,) * n)


def _sigmoid(x):
    return 1.0 / (1.0 + jnp.exp(-x))


def _silu(x):
    return x * _sigmoid(x)


def _gelu(x):
    return 0.5 * x * (1.0 + jnp.tanh(np.sqrt(2.0 / np.pi).astype(np.float32) * (x + 0.044715 * (x * x * x))))


def _rmsnorm(x, g):
    return x * lax.rsqrt(jnp.mean(x * x, -1, keepdims=True) + EPS) * g


def _layernorm(x, g, b):
    xc = x - jnp.mean(x, -1, keepdims=True)
    return xc * lax.rsqrt(jnp.mean(xc * xc, -1, keepdims=True) + EPS) * g + b


def _modulate(x, g, shift, scale):
    return _rmsnorm(x, g) * (1.0 + scale) + shift


def _dot(a, b):
    return jnp.dot(a, b, preferred_element_type=F32)


def _dot_nt(a, b):
    return lax.dot_general(a, b, (((1,), (1,)), ((), ())), preferred_element_type=F32)


def _lane_iota(shape):
    return lax.broadcasted_iota(jnp.int32, shape, len(shape) - 1)


def _masked_softmax(s, mask):
    s = jnp.where(mask, s, NEG)
    m = jnp.max(s, -1, keepdims=True)
    p = jnp.where(mask, jnp.exp(s - m), 0.0)
    return p / jnp.maximum(jnp.sum(p, -1, keepdims=True), 1e-30)


def _rope_pair(xs, cos, sin_signed):
    half = HEAD_DIM // 2
    first = (_lane_iota(xs.shape) & (HEAD_DIM - 1)) < half
    swapped = jnp.where(first, pltpu.roll(xs, LANES - half, 1), pltpu.roll(xs, half, 1))
    return xs * cos + swapped * sin_signed


def _select_blocks(imp, pos, n_top):
    lane = _lane_iota(imp.shape)
    j = lane & (SEL_LANES - 1)
    cur = pos >> SEL_SHIFT
    valid = j * SEL_BLOCK <= pos
    forced = (j == 0) | (j == cur) | (j == cur - 1)
    score = jnp.where(valid, jnp.where(forced, FORCED, imp), NEG)
    rank = jnp.zeros(imp.shape, F32)
    for d in range(1, SEL_LANES):
        nonwrap = j >= d
        rot = jnp.where(nonwrap, pltpu.roll(score, d, 1), pltpu.roll(score, (d - SEL_LANES) % LANES, 1))
        ahead = (rot > score) | ((rot == score) & nonwrap)
        rank = rank + jnp.where(ahead, 1.0, 0.0)
    return jnp.where(rank < n_top, 1.0, 0.0)


def _select_blocks_t(imp_t, pos, n_top):
    row = lax.broadcasted_iota(jnp.int32, (LANES, 1), 0)
    j = row & (SEL_LANES - 1)
    cur = pos >> SEL_SHIFT
    valid = j * SEL_BLOCK <= pos
    forced = (j == 0) | (j == cur) | (j == cur - 1)
    score = jnp.where(valid, jnp.where(forced, FORCED, imp_t), NEG)
    jj = lax.broadcasted_iota(jnp.int32, (SEL_LANES, 1), 0)
    out = []
    for k in range(N_KV):
        sc = score[k * SEL_LANES:(k + 1) * SEL_LANES]
        rank = jnp.zeros(sc.shape, F32)
        for i in range(SEL_LANES):
            other = sc[i:i + 1]
            tie_first = jnp.where(jj > i, 1.0, 0.0)
            rank = rank + jnp.where(other > sc, 1.0, 0.0) + jnp.where(other == sc, tie_first, 0.0)
        out.append(jnp.where(rank < n_top, 1.0, 0.0))
    return jnp.concatenate(out, axis=0)


def _ada_body(c_ref, w_ref, b_ref, o_ref):
    a = _silu(c_ref[...])
    o_ref[0, 0] = jnp.dot(a, w_ref[0], preferred_element_type=F32, precision=HIGHEST) + b_ref[0, 0]


def _ada_all(c_all, ada_w, ada_b):
    depth = ada_w.shape[0]
    rows = c_all.shape[0]
    return pl.pallas_call(
        _ada_body,
        out_shape=jax.ShapeDtypeStruct((depth, 6, rows, D_MODEL), F32),
        grid=(depth, 6),
        in_specs=[
            _const_spec((rows, D_MODEL)),
            pl.BlockSpec((1, D_MODEL, D_MODEL), lambda l, j: (l, 0, j)),
            pl.BlockSpec((1, 1, 1, D_MODEL), lambda l, j: (l, j, 0, 0)),
        ],
        out_specs=pl.BlockSpec((1, 1, rows, D_MODEL), lambda l, j: (l, j, 0, 0)),
        compiler_params=_cparams("arbitrary", "arbitrary"),
        name="ada_params",
    )(c_all, ada_w, ada_b.reshape(depth, 6, 1, D_MODEL))


def _post_body(x_ref, y_ref, mod_ref, gffn_ref, wo_ref, w1_ref, w2_ref, gfin_ref, o_ref, *, final, ff_chunk):
    x = x_ref[0]
    x = x + mod_ref[2, 0] * _dot(y_ref[0], wo_ref[...])
    h = _modulate(x, gffn_ref[...], mod_ref[3, 0], mod_ref[4, 0]).astype(BF16)
    acc = jnp.zeros(x.shape, F32)
    for c in range(D_FF // ff_chunk):
        t = _dot(h, w1_ref[:, c * ff_chunk:(c + 1) * ff_chunk])
        t = jnp.square(jnp.maximum(t, 0.0)).astype(BF16)
        acc = acc + _dot(t, w2_ref[c * ff_chunk:(c + 1) * ff_chunk, :])
    x = x + mod_ref[5, 0] * acc
    if final:
        x = _rmsnorm(x, gfin_ref[...])
    o_ref[0] = x


def _post(x, y, mods, g_ffn, w_out, w1, w2, g_fin, *, final, tm):
    bn, ln, _ = x.shape
    d_in = y.shape[-1]
    r = mods.shape[2]
    tm = min(tm, ln)
    mod_rows = 1 if r == 1 else tm
    mod_map = (lambda b, t: (0, b, 0, 0)) if r == 1 else (lambda b, t: (0, b, t, 0))
    return pl.pallas_call(
        functools.partial(_post_body, final=final, ff_chunk=1024),
        out_shape=jax.ShapeDtypeStruct(x.shape, F32),
        grid=(bn, ln // tm),
        in_specs=[
            pl.BlockSpec((1, tm, D_MODEL), lambda b, t: (b, t, 0)),
            pl.BlockSpec((1, tm, d_in), lambda b, t: (b, t, 0)),
            pl.BlockSpec((6, 1, mod_rows, D_MODEL), mod_map),
            _const_spec((1, D_MODEL)),
            _const_spec((d_in, D_MODEL)),
            _const_spec((D_MODEL, D_FF)),
            _const_spec((D_FF, D_MODEL)),
            _const_spec((1, D_MODEL)),
        ],
        out_specs=pl.BlockSpec((1, tm, D_MODEL), lambda b, t: (b, t, 0)),
        compiler_params=_cparams("arbitrary", "arbitrary"),
        name="post_mlp",
    )(x, y, mods, g_ffn, w_out, w1, w2, g_fin)


def _even_prompt_body(x_ref, mod_ref, gmix_ref, win_ref, cw_ref, cb_ref, clg_ref, clb_ref, slg_ref, slb_ref,
                      sw_ref, sbt_ref, y_ref, conv_ref, chv_ref, ext_ref, sh_ref, *, tm, conv_rows):
    t = pl.program_id(1)
    h = _modulate(x_ref[0], gmix_ref[...], mod_ref[0, 0], mod_ref[1, 0]).astype(BF16)
    z = _dot(h, win_ref[...])
    a = z[:, :D_A] * _sigmoid(z[:, D_A:2 * D_A])
    u = _gelu(z[:, 2 * D_A:2 * D_A + D_B])
    v = _layernorm(_gelu(z[:, 2 * D_A + D_B:]), slg_ref[...], slb_ref[...])

    @pl.when(t == 0)
    def _():
        ext_ref[0:CONV_HALO, :] = jnp.zeros((CONV_HALO, D_A), F32)

    ext_ref[CONV_HALO:CONV_HALO + tm, :] = a
    first = CONV_HALO - (CONV_WIDTH - 1)
    span = sh_ref.shape[1]
    for sft in range(1, SUBLANES):
        sh_ref[sft - 1] = ext_ref[sft:sft + span, :]
    pieces = []
    for c in range(tm // conv_rows):
        acc = jnp.zeros((conv_rows, D_A), F32) + cb_ref[...]
        for w in range(CONV_WIDTH):
            sft = (first + w) % SUBLANES
            r0 = first + w - sft + c * conv_rows
            src = ext_ref[r0:r0 + conv_rows, :] if sft == 0 else sh_ref[sft - 1, r0:r0 + conv_rows, :]
            acc = acc + src * cw_ref[w:w + 1, :]
        pieces.append(acc)
    a_conv = jnp.concatenate(pieces, axis=0)
    conv_ref[0] = ext_ref[tm + first:tm + CONV_HALO, :]
    ext_ref[0:CONV_HALO, :] = ext_ref[tm:tm + CONV_HALO, :]
    a_out = _silu(_layernorm(a_conv, clg_ref[...], clb_ref[...]))

    row = lax.broadcasted_iota(jnp.int32, (CHUNK, CHUNK), 0)
    col = lax.broadcasted_iota(jnp.int32, (CHUNK, CHUNK), 1)
    wm = [jnp.where(col <= row, sw_ref[g], 0.0).astype(BF16) for g in range(B_GROUPS)]
    vb = v.astype(BF16)
    gate_rows = []
    for c in range(tm // CHUNK):
        cols = []
        for g in range(B_GROUPS):
            vg = vb[c * CHUNK:(c + 1) * CHUNK, g * B_GROUP_DIM:(g + 1) * B_GROUP_DIM]
            cols.append(_dot(wm[g], vg) + sbt_ref[:, g:g + 1])
        gate_rows.append(jnp.concatenate(cols, axis=1))
    b_out = u * jnp.concatenate(gate_rows, axis=0)

    y_ref[0, :, :D_A] = a_out.astype(BF16)
    y_ref[0, :, D_A:] = b_out.astype(BF16)
    chv_ref[0] = v[tm - CHUNK:, :]


def _even_prompt(x, mods, g_mix, w_in, conv_w, conv_b, cln_g, cln_b, sln_g, sln_b, sgu_w, sgu_bt, *, tm):
    bn, ln, _ = x.shape
    assert ln % tm == 0 and tm % CHUNK == 0 and ln >= CONV_HALO
    return pl.pallas_call(
        functools.partial(_even_prompt_body, tm=tm, conv_rows=64),
        out_shape=(
            jax.ShapeDtypeStruct((bn, ln, D_MODEL), BF16),
            jax.ShapeDtypeStruct((bn, CONV_WIDTH - 1, D_A), F32),
            jax.ShapeDtypeStruct((bn, CHUNK, D_B), F32),
        ),
        grid=(bn, ln // tm),
        in_specs=[
            pl.BlockSpec((1, tm, D_MODEL), lambda b, t: (b, t, 0)),
            pl.BlockSpec((6, 1, 1, D_MODEL), lambda b, t: (0, b, 0, 0)),
            _const_spec((1, D_MODEL)),
            _const_spec((D_MODEL, 2 * D_A + 2 * D_B)),
            _const_spec((CONV_WIDTH, D_A)),
            _const_spec((1, D_A)), _const_spec((1, D_A)), _const_spec((1, D_A)),
            _const_spec((1, D_B)), _const_spec((1, D_B)),
            _const_spec((B_GROUPS, CHUNK, CHUNK)),
            _const_spec((CHUNK, B_GROUPS)),
        ],
        out_specs=(
            pl.BlockSpec((1, tm, D_MODEL), lambda b, t: (b, t, 0)),
            pl.BlockSpec((1, CONV_WIDTH - 1, D_A), lambda b, t: (b, 0, 0)),
            pl.BlockSpec((1, CHUNK, D_B), lambda b, t: (b, 0, 0)),
        ),
        scratch_shapes=[pltpu.VMEM((tm + CONV_HALO, D_A), F32),
                        pltpu.VMEM((SUBLANES - 1, tm + CONV_HALO - SUBLANES, D_A), F32)],
        compiler_params=_cparams("arbitrary", "arbitrary"),
        name="even_prompt",
    )(x, mods, g_mix, w_in, conv_w, conv_b, cln_g, cln_b, sln_g, sln_b, sgu_w, sgu_bt)


def _even_sample_body(x_ref, mod_ref, gmix_ref, win_ref, st_ref, cw_ref, cb_ref, clg_ref, clb_ref, slg_ref,
                      slb_ref, w00_ref, b0_ref, y_ref, a_ref, v_ref):
    h = _modulate(x_ref[...], gmix_ref[...], mod_ref[0], mod_ref[1]).astype(BF16)
    z = _dot(h, win_ref[...])
    a = z[:, :D_A] * _sigmoid(z[:, D_A:2 * D_A])
    u = _gelu(z[:, 2 * D_A:2 * D_A + D_B])
    v = _layernorm(_gelu(z[:, 2 * D_A + D_B:]), slg_ref[...], slb_ref[...])
    acc = a * cw_ref[CONV_WIDTH - 1:CONV_WIDTH, :] + cb_ref[...]
    for w in range(CONV_WIDTH - 1):
        acc = acc + st_ref[w] * cw_ref[w:w + 1, :]
    a_out = _silu(_layernorm(acc, clg_ref[...], clb_ref[...]))
    b_out = u * (w00_ref[...] * v + b0_ref[...])
    y_ref[:, :D_A] = a_out.astype(BF16)
    y_ref[:, D_A:] = b_out.astype(BF16)
    a_ref[...] = a
    v_ref[...] = v


def _even_sample(x, mods, g_mix, w_in, state_t, conv_w, conv_b, cln_g, cln_b, sln_g, sln_b, w00, b0):
    sn = x.shape[0]
    shapes = [a.shape for a in (x, mods, g_mix, w_in, state_t, conv_w, conv_b, cln_g, cln_b, sln_g, sln_b, w00, b0)]
    return pl.pallas_call(
        _even_sample_body,
        out_shape=(
            jax.ShapeDtypeStruct((sn, D_MODEL), BF16),
            jax.ShapeDtypeStruct((sn, D_A), F32),
            jax.ShapeDtypeStruct((sn, D_B), F32),
        ),
        grid=(1,),
        in_specs=[_const_spec(s) for s in shapes],
        out_specs=(_const_spec((sn, D_MODEL)), _const_spec((sn, D_A)), _const_spec((sn, D_B))),
        compiler_params=_cparams("arbitrary"),
        name="even_sample",
    )(x, mods, g_mix, w_in, state_t, conv_w, conv_b, cln_g, cln_b, sln_g, sln_b, w00, b0)


def _proj_body(x_ref, mod_ref, gmix_ref, w_ref, cos_ref, sin_ref,
               qs_ref, qrs_ref, c0_ref, c1_ref, c2_ref, c3_ref, rows_ref, rowsb_ref, cols_ref, colsb_ref, g_ref):
    tm = x_ref.shape[1]
    h = _modulate(x_ref[0], gmix_ref[...], mod_ref[0, 0], mod_ref[1, 0]).astype(BF16)
    z = _dot(h, w_ref[...])
    cos = cos_ref[...]
    sin = sin_ref[...]
    q0 = N_HEADS * HEAD_DIM
    low = _lane_iota((tm, LANES)) < HEAD_DIM
    for c in range(q0 // LANES):
        xs = z[:, c * LANES:(c + 1) * LANES] * SCALE
        xr = _rope_pair(xs, cos, sin)
        for half in range(2):
            hd = 2 * c + half
            par = (hd // GROUP) % 2
            keep = low if par == 0 else jnp.logical_not(low)
            for src, ref in ((xs, qs_ref), (xr, qrs_ref)):
                v = src if half == par else pltpu.roll(src, HEAD_DIM, 1)
                ref[0, hd] = jnp.where(keep, v, 0.0).astype(BF16)

    def kv(i):
        return z[:, q0 + i * KV_DIM:q0 + (i + 1) * KV_DIM]

    def kv_rot(i):
        base = q0 + i * KV_DIM
        return jnp.concatenate(
            [_rope_pair(z[:, base + c * LANES:base + (c + 1) * LANES], cos, sin) for c in range(N_PAIR)], axis=1)

    rows = jnp.concatenate([kv(0), kv(1), kv_rot(2), kv(3), kv_rot(4), kv(5)], axis=1)
    for i, ref in enumerate((c0_ref, c1_ref, c2_ref, c3_ref)):
        ref[0] = rows[:, i * LANES:(i + 1) * LANES]
    rows_ref[0] = rows[:, 2 * KV_DIM:]
    rowsb_ref[0] = rows[:, 2 * KV_DIM:].astype(BF16)
    cols = rows.T
    cols_ref[0] = cols
    colsb_ref[0] = cols[2 * KV_DIM:].astype(BF16)
    g_ref[0] = _sigmoid(z[:, q0 + 6 * KV_DIM:])


def _nsa_project(x, mods, g_mix, w_in, cos, sin, *, tm):
    bn, ln, _ = x.shape
    r = mods.shape[2]
    tm = min(tm, ln)
    mod_rows = 1 if r == 1 else tm
    mod_map = (lambda b, t: (0, b, 0, 0)) if r == 1 else (lambda b, t: (0, b, t, 0))
    tok = lambda w: pl.BlockSpec((1, tm, w), lambda b, t: (b, t, 0))
    col = lambda w: pl.BlockSpec((1, w, tm), lambda b, t: (b, 0, t))
    head = pl.BlockSpec((1, N_HEADS, tm, LANES), lambda b, t: (b, 0, t, 0))
    return pl.pallas_call(
        _proj_body,
        out_shape=(
            jax.ShapeDtypeStruct((bn, N_HEADS, ln, LANES), BF16),
            jax.ShapeDtypeStruct((bn, N_HEADS, ln, LANES), BF16),
            jax.ShapeDtypeStruct((bn, ln, LANES), F32),
            jax.ShapeDtypeStruct((bn, ln, LANES), F32),
            jax.ShapeDtypeStruct((bn, ln, LANES), F32),
            jax.ShapeDtypeStruct((bn, ln, LANES), F32),
            jax.ShapeDtypeStruct((bn, ln, 4 * KV_DIM), F32),
            jax.ShapeDtypeStruct((bn, ln, 4 * KV_DIM), BF16),
            jax.ShapeDtypeStruct((bn, 6 * KV_DIM, ln), F32),
            jax.ShapeDtypeStruct((bn, 4 * KV_DIM, ln), BF16),
            jax.ShapeDtypeStruct((bn, ln, LANES), F32),
        ),
        grid=(bn, ln // tm),
        in_specs=[
            tok(D_MODEL),
            pl.BlockSpec((6, 1, mod_rows, D_MODEL), mod_map),
            _const_spec((1, D_MODEL)),
            _const_spec((D_MODEL, ODD_IN_PAD)),
            pl.BlockSpec((tm, LANES), lambda b, t: (t, 0)),
            pl.BlockSpec((tm, LANES), lambda b, t: (t, 0)),
        ],
        out_specs=(head, head, tok(LANES), tok(LANES), tok(LANES), tok(LANES), tok(4 * KV_DIM), tok(4 * KV_DIM),
                   col(6 * KV_DIM), col(4 * KV_DIM), tok(LANES)),
        compiler_params=_cparams("arbitrary", "arbitrary"),
        name="nsa_project",
    )(x, mods, g_mix, w_in, cos, sin)


def _compress_rows(load_rows, pe2_ref, w1_ref, w2e_ref, w2o_ref):
    n_chunk = 128
    lane = _lane_iota((n_chunk, LANES))
    low = lane < HEAD_DIM
    heads = [[] for _ in range(N_KV)]
    for p in range(CMP_STRIDE // 2):
        for m in range(N_PAIR):
            lo = load_rows(2 * p, m)
            hi = load_rows(2 * p + 1, m)
            heads[2 * m].append(jnp.where(low, lo, pltpu.roll(hi, HEAD_DIM, 1)))
            heads[2 * m + 1].append(jnp.where(low, pltpu.roll(lo, HEAD_DIM, 1), hi))
    x_all = jnp.concatenate([jnp.concatenate(hp, axis=1) for hp in heads], axis=0).astype(BF16)
    ab = _dot(x_all, w1_ref[...])
    pe = _dot(pe2_ref[...], w1_ref[...])
    hid0 = pe[0:1, :CMP_HIDDEN] + pe[1:2, CMP_HIDDEN:]
    hid = []
    for k in range(N_KV):
        a_part = ab[k * n_chunk:(k + 1) * n_chunk, :CMP_HIDDEN]
        b_part = ab[k * n_chunk:(k + 1) * n_chunk, CMP_HIDDEN:]
        nxt = pltpu.roll(b_part, n_chunk - 1, 0)
        hid.append(_gelu(a_part + nxt + hid0).astype(BF16))
    out = []
    for m in range(N_PAIR):
        out.append(_dot(hid[2 * m], w2e_ref[...]) + _dot(hid[2 * m + 1], w2o_ref[...]))
    return jnp.concatenate(out, axis=1)


def _compress_prompt_body(c0_ref, c1_ref, c2_ref, c3_ref, pek_ref, w1k_ref, w2ke_ref, w2ko_ref,
                          pev_ref, w1v_ref, w2ve_ref, w2vo_ref, kc_ref, vc_ref):
    slabs = (c0_ref, c1_ref, c2_ref, c3_ref)
    n_chunk = c0_ref.shape[1] // CMP_STRIDE
    assert n_chunk == 128

    def rows(first):
        return lambda j, m: slabs[first + m][0, pl.ds(j, n_chunk, stride=CMP_STRIDE), :]

    kc_ref[0] = _compress_rows(rows(0), pek_ref, w1k_ref, w2ke_ref, w2ko_ref).astype(BF16)
    vc_ref[0] = _compress_rows(rows(N_PAIR), pev_ref, w1v_ref, w2ve_ref, w2vo_ref).astype(BF16)


def _compress_prompt(slabs, cw_k, cw_v):
    bn, ln, _ = slabs[0].shape
    wspecs = [_const_spec(a.shape) for a in cw_k + cw_v]
    seq = pl.BlockSpec((1, ln, LANES), lambda b: (b, 0, 0))
    out = pl.BlockSpec((1, 128, KV_DIM), lambda b: (b, 0, 0))
    return pl.pallas_call(
        _compress_prompt_body,
        out_shape=(jax.ShapeDtypeStruct((bn, 128, KV_DIM), BF16),) * 2,
        grid=(bn,),
        in_specs=[seq] * 4 + wspecs,
        out_specs=(out, out),
        compiler_params=_cparams("arbitrary"),
        name="compress_prompt",
    )(*slabs, *cw_k, *cw_v)


ONES_ROWS = 16


def _flash_step(carry, s, v1):
    m_i, acc = carry
    m_new = jnp.maximum(m_i, jnp.max(s, 0, keepdims=True))
    alpha = jnp.exp(m_i - m_new)
    p = jnp.exp(s - m_new)
    return m_new, alpha * acc + _dot(v1, p.astype(BF16))


def _attn_prompt_body(qs_ref, qrs_ref, g_ref, kc_ref, vc_ref, krow_ref, kvt_ref, ovt_ref, nege_ref, o_ref, *, t):
    qi = pl.program_id(1)
    q0 = pl.multiple_of(qi * t, t)
    rows4 = GROUP * t
    pos_t = q0 + (_lane_iota((1, rows4)) & (t - 1))
    gates_t = g_ref[0].T

    def q_t(ref, k):
        return ref[0, GROUP * k:GROUP * (k + 1)].reshape(rows4, LANES).astype(F32).T.astype(BF16)

    def per_token(x):
        return jnp.concatenate([x] * GROUP, axis=1)

    key_n = lax.broadcasted_iota(jnp.int32, (LANES, 1), 0)
    cmask = (key_n < LANES - 1) & (key_n * CMP_STRIDE + (CMP_BLOCK - 1) <= pos_t)
    o_cmp = []
    imp_t = jnp.zeros((LANES, t), F32)
    for k in range(N_KV):
        m = k // 2
        s = jnp.where(cmask, _dot(kc_ref[0, :, m * LANES:(m + 1) * LANES], q_t(qs_ref, k)), NEG)
        p = jnp.where(cmask, jnp.exp(s - jnp.max(s, 0, keepdims=True)), 0.0)
        p = p / jnp.maximum(jnp.sum(p, 0, keepdims=True), 1e-30)
        o_cmp.append(_dot(vc_ref[0, :, m * LANES:(m + 1) * LANES].astype(F32).T.astype(BF16), p.astype(BF16)))
        psum = p[:, 0:t] + p[:, t:2 * t] + p[:, 2 * t:3 * t] + p[:, 3 * t:4 * t]
        imp_t = imp_t + jnp.dot(ovt_ref[k], psum, preferred_element_type=F32, precision=HIGHEST)
    unsel_t = per_token((1.0 - _select_blocks_t(imp_t, pos_t[:, :t], N_SEL)).astype(BF16))
    ones = jnp.ones((ONES_ROWS, t), BF16)

    kr = lax.broadcasted_iota(jnp.int32, (t, t), 0)
    qc = lax.broadcasted_iota(jnp.int32, (t, t), 1)
    bias_diag = per_token(jnp.where(kr <= qc, 0.0, MASK_BIAS))
    bias_edge = per_token(jnp.where(kr >= qc, 0.0, MASK_BIAS))

    def k_tile(stream, m, kt):
        k0 = pl.multiple_of(kt * t, t)
        base = stream * KV_DIM + m * LANES
        return krow_ref[0, pl.ds(k0, t), base:base + LANES]

    def v_ones(stream, m, kt):
        k0 = pl.multiple_of(kt * t, t)
        base = stream * KV_DIM + m * LANES
        return jnp.concatenate([kvt_ref[0, base:base + LANES, pl.ds(k0, t)], ones], axis=0)

    def init():
        return jnp.full((1, rows4), NEG, F32), jnp.zeros((LANES + ONES_ROWS, rows4), F32)

    def finish(carry):
        _, acc = carry
        return acc[:LANES] / jnp.maximum(acc[LANES:LANES + 1], 1e-30)

    heads = [None] * N_HEADS
    for k in range(N_KV):
        m = k // 2
        dims = slice((k % 2) * HEAD_DIM, (k % 2 + 1) * HEAD_DIM)
        qr_t = q_t(qrs_ref, k)

        qu_t = jnp.concatenate([qr_t, unsel_t], axis=0)

        def sel_scores(kt, m=m, k=k, qu_t=qu_t):
            k0 = pl.multiple_of(kt * t, t)
            return _dot(jnp.concatenate([k_tile(0, m, kt), nege_ref[k, pl.ds(k0, t), :]], axis=1), qu_t)

        def sel_step(kt, cr, m=m, sel_scores=sel_scores):
            return _flash_step(cr, sel_scores(kt), v_ones(1, m, kt))

        def sel_pair(i, cr, sel_step=sel_step):
            return sel_step(2 * i + 1, sel_step(2 * i, cr))

        carry = lax.fori_loop(0, qi // 2, sel_pair, init())
        carry = lax.cond(qi % 2 == 1, lambda cr: sel_step(qi - 1, cr), lambda cr: cr, carry)
        carry = _flash_step(carry, sel_scores(qi) + bias_diag, v_ones(1, m, qi))
        o_sel = finish(carry)

        carry = init()
        for back, bias in ((2, bias_edge), (1, None)):
            def visit(cr, back=back, bias=bias, m=m, qr_t=qr_t):
                s = _dot(k_tile(2, m, qi - back), qr_t)
                return _flash_step(cr, s if bias is None else s + bias, v_ones(3, m, qi - back))
            carry = lax.cond(qi >= back, visit, lambda cr: cr, carry)
        carry = _flash_step(carry, _dot(k_tile(2, m, qi), qr_t) + bias_diag, v_ones(3, m, qi))
        o_win = finish(carry)

        for g in range(GROUP):
            h = GROUP * k + g
            cols = slice(g * t, (g + 1) * t)
            heads[h] = (gates_t[3 * h:3 * h + 1] * o_cmp[k][dims, cols] + gates_t[3 * h + 1:3 * h + 2] * o_sel[dims, cols]
                        + gates_t[3 * h + 2:3 * h + 3] * o_win[dims, cols])
    o_ref[0] = jnp.concatenate(heads, axis=0).T.astype(BF16)


def _attn_prompt(qs, qrs, gates, kc, vc, krow, kvt, ov_t, nege, *, t):
    bn, _, ln, _ = qs.shape
    assert ln % t == 0 and ln <= SEL_LANES * SEL_BLOCK and WINDOW == 2 * t and t & (t - 1) == 0
    head = pl.BlockSpec((1, N_HEADS, t, LANES), lambda b, i: (b, 0, i, 0))
    cmp_spec = pl.BlockSpec((1, 128, KV_DIM), lambda b, i: (b, 0, 0))
    return pl.pallas_call(
        functools.partial(_attn_prompt_body, t=t),
        out_shape=jax.ShapeDtypeStruct((bn, ln, D_MODEL), BF16),
        grid=(bn, ln // t),
        in_specs=[head, head, pl.BlockSpec((1, t, LANES), lambda b, i: (b, i, 0)), cmp_spec, cmp_spec,
                  pl.BlockSpec((1, ln, 4 * KV_DIM), lambda b, i: (b, 0, 0)),
                  pl.BlockSpec((1, 4 * KV_DIM, ln), lambda b, i: (b, 0, 0)),
                  _const_spec(ov_t.shape), _const_spec(nege.shape)],
        out_specs=pl.BlockSpec((1, t, D_MODEL), lambda b, i: (b, i, 0)),
        compiler_params=_cparams("arbitrary", "arbitrary"),
        name="attn_prompt",
    )(qs, qrs, gates, kc, vc, krow, kvt, ov_t, nege)


def _attn_sample_body(pt_ref, qs_ref, qrs_ref, g_ref, new_ref, wkc_ref, wvc_ref, wks_ref, wvs_ref, *rest, n_pages):
    del pt_ref
    pools = [rest[i * n_pages:(i + 1) * n_pages] for i in range(4)]
    (pek_ref, w1k_ref, w2ke_ref, w2ko_ref, pev_ref, w1v_ref, w2ve_ref, w2vo_ref, ov_ref, expand_ref,
     o_ref, wko_ref, wvo_ref, x_ref) = rest[4 * n_pages:]
    past = n_pages * PAGE_SIZE

    def compress(pages, *weights):
        pitch = CMP_STRIDE + 1
        for i, pg in enumerate(pages):
            for m in range(N_PAIR):
                rows_t = pg[0, 0, m * LANES:(m + 1) * LANES, :].T
                for c in range(PAGE_SIZE // CMP_STRIDE):
                    r0 = (i * (PAGE_SIZE // CMP_STRIDE) + c) * pitch
                    x_ref[m, r0:r0 + CMP_STRIDE, :] = rows_t[c * CMP_STRIDE:(c + 1) * CMP_STRIDE]
        load = lambda j, m: x_ref[m, pl.ds(j, past // CMP_STRIDE, stride=pitch), :]
        return _compress_rows(load, *weights).astype(BF16)

    kc = compress(pools[0], pek_ref, w1k_ref, w2ke_ref, w2ko_ref)
    vc = compress(pools[1], pev_ref, w1v_ref, w2ve_ref, w2vo_ref)

    lane = _lane_iota((SUBLANES, LANES))
    upper_rows = lax.broadcasted_iota(jnp.int32, (SUBLANES, LANES), 0) >= GROUP
    gates = g_ref[0]
    pos = jnp.full((SUBLANES, 1), past, jnp.int32)

    cmask = lane < LANES - 1
    o_cmp = []
    imp = jnp.zeros((SUBLANES, LANES), F32)
    for m in range(N_PAIR):
        qs = qs_ref[0, 2 * GROUP * m:2 * GROUP * (m + 1), :]
        p = _masked_softmax(_dot_nt(qs, kc[:, m * LANES:(m + 1) * LANES]), cmask)
        o_cmp.append(_dot(p.astype(BF16), vc[:, m * LANES:(m + 1) * LANES]))
        for kk in range(2):
            psum = jnp.sum(p[kk * GROUP:(kk + 1) * GROUP], axis=0, keepdims=True)
            imp = imp + jnp.dot(jnp.broadcast_to(psum, (SUBLANES, LANES)), ov_ref[2 * m + kk],
                                preferred_element_type=F32, precision=HIGHEST)
    sel = _select_blocks(imp, pos, N_SEL - 1)
    seg = lane >> SEL_LANES_SHIFT
    expand = expand_ref[...]

    def new_row(stream, m):
        base = stream * KV_DIM + m * LANES
        return new_ref[0, :, base:base + LANES]

    for m in range(N_PAIR):
        rows = slice(m * LANES, (m + 1) * LANES)
        qrs = qrs_ref[0, 2 * GROUP * m:2 * GROUP * (m + 1), :]
        qrf = qrs.astype(F32)
        want = jnp.where(upper_rows, 2 * m + 1, 2 * m)
        sel_rows = jnp.where(seg == want, sel, 0.0).astype(BF16)
        ok = _dot(sel_rows, expand) > 0.5
        s = jnp.concatenate([_dot(qrs, pg[0, 0, rows, :].astype(BF16)) for pg in pools[2]], axis=1)
        s_new = jnp.sum(qrf * new_row(0, m), -1, keepdims=True)
        s = jnp.where(ok, s, NEG)
        mx = jnp.maximum(jnp.max(s, -1, keepdims=True), s_new)
        p = jnp.where(ok, jnp.exp(s - mx), 0.0)
        p_new = jnp.exp(s_new - mx)
        denom = jnp.maximum(jnp.sum(p, -1, keepdims=True) + p_new, 1e-30)
        pb = p.astype(BF16)
        acc = p_new * new_row(1, m)
        for i, pg in enumerate(pools[3]):
            acc = acc + _dot_nt(pb[:, i * PAGE_SIZE:(i + 1) * PAGE_SIZE], pg[0, 0, rows, :].astype(BF16))
        o_sel = acc / denom
        sw = _dot(qrs, wks_ref[0, 0, rows, :].astype(BF16))
        sw_new = jnp.sum(qrf * new_row(2, m), -1, keepdims=True)
        mw = jnp.maximum(jnp.max(sw, -1, keepdims=True), sw_new)
        pw = jnp.exp(sw - mw)
        pw_new = jnp.exp(sw_new - mw)
        dw = jnp.maximum(jnp.sum(pw, -1, keepdims=True) + pw_new, 1e-30)
        o_win = (_dot_nt(pw.astype(BF16), wvs_ref[0, 0, rows, :].astype(BF16)) + pw_new * new_row(3, m)) / dw
        for kk in range(2):
            for c in range(GROUP // 2):
                halves = []
                for gg in range(2):
                    g = 2 * c + gg
                    h = GROUP * (2 * m + kk) + g
                    r = kk * GROUP + g
                    o = (gates[:, 3 * h:3 * h + 1] * o_cmp[m][r:r + 1] + gates[:, 3 * h + 1:3 * h + 2] * o_sel[r:r + 1]
                         + gates[:, 3 * h + 2:3 * h + 3] * o_win[r:r + 1])
                    if gg != kk:
                        o = pltpu.roll(jnp.broadcast_to(o, (SUBLANES, LANES)), HEAD_DIM, 1)[0:1]
                    halves.append(o)
                col = (GROUP * (2 * m + kk)) // 2 + c
                o_ref[0, :, col * LANES:(col + 1) * LANES] = jnp.where(
                    _lane_iota((1, LANES)) < HEAD_DIM, halves[0], halves[1]).astype(BF16)

    nbuf = wks_ref.shape[-1]
    last = _lane_iota((KV_DIM, nbuf)) == nbuf - 1
    wko_ref[0] = jnp.where(last, wkc_ref[0], pltpu.roll(wks_ref[0, 0], nbuf - 1, 1))
    wvo_ref[0] = jnp.where(last, wvc_ref[0], pltpu.roll(wvs_ref[0, 0], nbuf - 1, 1))


def _attn_sample(page_table, layer, qs, qrs, gates, new_rows, wk_col, wv_col, win_k, win_v,
                 pool_ck, pool_cv, pool_sk, pool_sv, cw_k, cw_v, ov):
    sn = qs.shape[0]
    n_pages = page_table.shape[1]
    nbuf = win_k.shape[-1]
    past = n_pages * PAGE_SIZE
    assert nbuf == WINDOW and past == SEL_LANES * SEL_BLOCK
    per_seq = lambda *blk: pl.BlockSpec((1,) + blk, lambda s, pt: (s,) + (0,) * len(blk))
    page_map = lambda s, pt, p: (layer, pt[s, p], 0, 0)
    page_specs = [pl.BlockSpec((1, 1, KV_DIM, PAGE_SIZE), functools.partial(page_map, p=p)) for p in range(n_pages)]
    win_spec = pl.BlockSpec((1, 1, KV_DIM, nbuf), lambda s, pt: (layer, s, 0, 0))
    expand = (np.arange(LANES)[:, None] % SEL_LANES) == (np.arange(past)[None, :] // SEL_BLOCK)
    weights = cw_k + cw_v + (ov, jnp.asarray(expand, dtype=BF16))
    grid_spec = pltpu.PrefetchScalarGridSpec(
        num_scalar_prefetch=1,
        grid=(sn,),
        in_specs=[per_seq(N_HEADS, LANES), per_seq(N_HEADS, LANES), per_seq(1, LANES), per_seq(1, 4 * KV_DIM),
                  per_seq(KV_DIM, 1), per_seq(KV_DIM, 1), win_spec, win_spec]
        + page_specs * 4
        + [pl.BlockSpec(w.shape, functools.partial(lambda s, pt, n: (0,) * n, n=w.ndim)) for w in weights],
        out_specs=(per_seq(1, D_MODEL), per_seq(KV_DIM, nbuf), per_seq(KV_DIM, nbuf)),
        scratch_shapes=[pltpu.VMEM((N_PAIR, past // CMP_STRIDE * (CMP_STRIDE + 1) + SUBLANES - 1, LANES), F32)],
    )
    pools = [pool_ck] * n_pages + [pool_cv] * n_pages + [pool_sk] * n_pages + [pool_sv] * n_pages
    return pl.pallas_call(
        functools.partial(_attn_sample_body, n_pages=n_pages),
        out_shape=(jax.ShapeDtypeStruct((sn, 1, D_MODEL), BF16),
                   jax.ShapeDtypeStruct((sn, KV_DIM, nbuf), F32),
                   jax.ShapeDtypeStruct((sn, KV_DIM, nbuf), F32)),
        grid_spec=grid_spec,
        compiler_params=_cparams("arbitrary"),
        name="attn_sample",
    )(page_table, qs, qrs, gates, new_rows, wk_col, wv_col, win_k, win_v, *pools, *weights)


def _rope_tables(pos):
    half = HEAD_DIM // 2
    inv = jnp.power(jnp.float32(ROPE_THETA), -jnp.arange(half, dtype=F32) * (2.0 / HEAD_DIM))
    ang = pos.astype(F32)[:, None] * inv[None, :]
    cos, sin = jnp.cos(ang), jnp.sin(ang)
    return jnp.concatenate([cos, cos, cos, cos], -1), jnp.concatenate([-sin, sin, -sin, sin], -1)


def _overlap_tables():
    n_cmp, n_sel = LANES - 1, SEL_LANES
    i = np.arange(n_cmp)[:, None] * CMP_STRIDE
    j = np.arange(n_sel)[None, :] * SEL_BLOCK
    ov = ((i < j + SEL_BLOCK) & (i + CMP_BLOCK > j)).astype(np.float32)
    out = np.zeros((N_KV, LANES, LANES), np.float32)
    for k in range(N_KV):
        out[k, :n_cmp, k * SEL_LANES:(k + 1) * SEL_LANES] = ov
    return jnp.asarray(out)


def _unselected_bias_table(ln):
    c = np.arange(ln)[:, None]
    r = np.arange(LANES)[None, :]
    out = np.zeros((N_KV, ln, LANES), np.float32)
    for k in range(N_KV):
        out[k] = np.where((r // SEL_LANES == k) & (r % SEL_LANES == c // SEL_BLOCK), MASK_BIAS, 0.0)
    return jnp.asarray(out, dtype=BF16)


def _rows_last(t):
    nd = t.ndim
    t = jnp.transpose(t, tuple(range(nd - 3)) + (nd - 2, nd - 1, nd - 3))
    return t.reshape(t.shape[:-3] + (KV_DIM, t.shape[-1]))


def _rows_first(t):
    nd = t.ndim
    t = t.reshape(t.shape[:-2] + (N_KV, HEAD_DIM, t.shape[-1]))
    return jnp.transpose(t, tuple(range(nd - 2)) + (nd, nd - 2, nd - 1))


def _compress_weights(pe, w1, w2):
    r = CMP_BLOCK // CMP_STRIDE
    flat = CMP_STRIDE * HEAD_DIM
    w1cat = jnp.concatenate([w1[m * CMP_STRIDE:(m + 1) * CMP_STRIDE].reshape(flat, CMP_HIDDEN) for m in range(r)], 1)
    pe2 = jnp.zeros((SUBLANES, flat), F32).at[:r].set(pe.reshape(r, flat))
    zeros = jnp.zeros_like(w2)
    return (pe2.astype(BF16), w1cat.astype(BF16), jnp.concatenate([w2, zeros], 1).astype(BF16),
            jnp.concatenate([zeros, w2], 1).astype(BF16))


def kernel(x_prompt, x_sample, state_conv, cache_cmp_k, cache_cmp_v, cache_sel_k, cache_sel_v, state_win_k, state_win_v, page_table, c_prompt, c_sample, ada_w, ada_b, norm_mix_g, norm_ffn_g, ffn_w1, ffn_w2, even_w_in, even_w_out, conv_w, conv_b, conv_ln_g, conv_ln_b, sgu_ln_g, sgu_ln_b, sgu_w, sgu_b, odd_w_in, odd_w_out, cmp_pe_k, cmp_w1_k, cmp_w2_k, cmp_pe_v, cmp_w1_v, cmp_w2_v, final_norm_g):
    depth = ada_w.shape[0]
    bn, ln, _ = x_prompt.shape
    sn = x_sample.shape[0]
    past = page_table.shape[1] * PAGE_SIZE

    ada = _ada_all(jnp.concatenate([c_prompt, c_sample], 0), ada_w, ada_b)
    ov = _overlap_tables()
    ov_t = jnp.swapaxes(ov, 1, 2)
    nege = _unselected_bias_table(ln)
    cos_p, sin_p = _rope_tables(jnp.arange(ln))
    cos_s, sin_s = _rope_tables(jnp.full((sn,), past))
    g_fin = final_norm_g.reshape(1, D_MODEL)
    row = lambda a: a.reshape(1, -1)
    pools = [_rows_last(c) for c in (cache_cmp_k, cache_cmp_v, cache_sel_k, cache_sel_v)]
    win_k_all, win_v_all = _rows_last(state_win_k), _rows_last(state_win_v)

    xp = x_prompt
    xs = x_sample.reshape(1, sn, D_MODEL)
    conv_p, conv_s, chv_p, chv_s = [], [], [], []
    cols_p, cols_s, win_s = [], [], []
    for l in range(depth):
        mods_p = ada[l, :, :bn].reshape(6, bn, 1, D_MODEL)
        mods_s = ada[l, :, bn:].reshape(6, 1, sn, D_MODEL)
        g_mix = row(norm_mix_g[l])
        if l % 2 == 0:
            e = l // 2
            w_in = even_w_in[e].astype(BF16)
            w_out = even_w_out[e].astype(BF16)
            cln = (row(conv_b[e]), row(conv_ln_g[e]), row(conv_ln_b[e]), row(sgu_ln_g[e]), row(sgu_ln_b[e]))
            yp, bp, vp = _even_prompt(xp, mods_p, g_mix, w_in, conv_w[e], *cln, sgu_w[e], sgu_b[e].T, tm=256)
            w00 = row(jnp.repeat(sgu_w[e, :, 0, 0], B_GROUP_DIM))
            b0 = row(jnp.repeat(sgu_b[e, :, 0], B_GROUP_DIM))
            ys, a_s, v_s = _even_sample(xs[0], mods_s[:, 0], g_mix, w_in, jnp.swapaxes(state_conv[e], 0, 1),
                                        conv_w[e], *cln, w00, b0)
            conv_p.append(bp)
            conv_s.append(jnp.concatenate([state_conv[e][:, 1:], a_s[:, None, :]], axis=1))
            chv_p.append(vp)
            chv_s.append(v_s[:, None, :])
            ys = ys[None]
        else:
            o = l // 2
            w_in = jnp.pad(odd_w_in[o], ((0, 0), (0, ODD_IN_PAD - odd_w_in.shape[-1]))).astype(BF16)
            w_out = odd_w_out[o].astype(BF16)
            cw_k = _compress_weights(cmp_pe_k[o], cmp_w1_k[o], cmp_w2_k[o])
            cw_v = _compress_weights(cmp_pe_v[o], cmp_w1_v[o], cmp_w2_v[o])
            qs, qrs, c0, c1, c2, c3, _, rowsb, cols, colsb, gates = _nsa_project(
                xp, mods_p, g_mix, w_in, cos_p, sin_p, tm=256)
            kc, vc = _compress_prompt((c0, c1, c2, c3), cw_k, cw_v)
            yp = _attn_prompt(qs, qrs, gates, kc, vc, rowsb, colsb, ov_t, nege, t=256)
            cols_p.append(cols)
            qs, qrs, _, _, _, _, rows, _, cols, _, gates = _nsa_project(xs, mods_s, g_mix, w_in, cos_s, sin_s, tm=sn)
            heads = lambda t: jnp.swapaxes(t[0], 0, 1)
            col = lambda i: rows[0, :, i * KV_DIM:(i + 1) * KV_DIM].reshape(sn, KV_DIM, 1)
            ys, wk_new, wv_new = _attn_sample(
                page_table, o, heads(qs), heads(qrs), gates.reshape(sn, 1, LANES), rows.reshape(sn, 1, 4 * KV_DIM),
                col(2), col(3), win_k_all, win_v_all, *pools, cw_k, cw_v, ov)
            ys = ys.reshape(1, sn, D_MODEL)
            cols_s.append(cols[0])
            win_s.append((wk_new, wv_new))
        final = l == depth - 1
        w1 = ffn_w1[l].astype(BF16)
        w2 = ffn_w2[l].astype(BF16)
        g_ffn = row(norm_ffn_g[l])
        xp = _post(xp, yp, mods_p, g_ffn, w_out, w1, w2, g_fin, final=final, tm=512)
        xs = _post(xs, ys, mods_s, g_ffn, w_out, w1, w2, g_fin, final=final, tm=sn)
    st = lambda lst: jnp.stack(lst, axis=0)
    nw = min(WINDOW, ln)
    cols_p = st(cols_p)
    stream_p = lambda i, lo: _rows_first(cols_p[:, :, i * KV_DIM:(i + 1) * KV_DIM, lo:])
    cols_s = st(cols_s)
    stream_s = lambda i: _rows_first(cols_s[:, i * KV_DIM:(i + 1) * KV_DIM, :])[:, :, None]
    return (xp, xs.reshape(sn, 1, D_MODEL),
            st(conv_p), st(conv_s), st(chv_p), st(chv_s),
            stream_p(0, 0), stream_p(1, 0), stream_p(2, 0), stream_p(3, 0), stream_p(4, ln - nw), stream_p(5, ln - nw),
            stream_s(0), stream_s(1), stream_s(2), stream_s(3),
            _rows_first(st([w[0] for w in win_s])), _rows_first(st([w[1] for w in win_s])))
```

```python
import functools

import numpy as np
import jax
import jax.numpy as jnp
from jax import lax
from jax.experimental import pallas as pl
from jax.experimental.pallas import tpu as pltpu

F32 = jnp.float32
BF16 = jnp.bfloat16
HIGHEST = lax.Precision.HIGHEST

LANES = 128
SUBLANES = 8
VMEM_LIMIT_BYTES = 56 * 1024 * 1024

D_MODEL = 1024
D_FF = 4 * D_MODEL
D_A = D_MODEL // 2
CONV_WIDTH = 31
CONV_HALO = 32
D_B = D_MODEL // 2
B_GROUPS = 4
B_GROUP_DIM = D_B // B_GROUPS
CHUNK = 128
N_HEADS = 16
HEAD_DIM = 64
N_KV = 4
GROUP = N_HEADS // N_KV
KV_DIM = N_KV * HEAD_DIM
N_PAIR = KV_DIM // LANES
CMP_BLOCK = 32
CMP_STRIDE = 16
CMP_HIDDEN = 2 * HEAD_DIM
SEL_BLOCK = 64
N_SEL = 8
SEL_LANES = 32
SEL_SHIFT = 6
SEL_LANES_SHIFT = 5
WINDOW = 512
PAGE_SIZE = 128
ROPE_THETA = 10000.0
EPS = 1e-6
NEG = -1e30
MASK_BIAS = -(2.0 ** 100)
FORCED = 1e4
SCALE = HEAD_DIM ** -0.5
ODD_IN_PAD = N_HEADS * HEAD_DIM + 6 * KV_DIM + LANES


def _cparams(*sem):
    return pltpu.CompilerParams(dimension_semantics=sem, vmem_limit_bytes=VMEM_LIMIT_BYTES)


def _const_spec(shape):
    n = len(shape)
    return pl.BlockSpec(shape, lambda *_: (0,) * n)


def _sigmoid(x):
    return 1.0 / (1.0 + jnp.exp(-x))


def _silu(x):
    return x * _sigmoid(x)


def _gelu(x):
    return 0.5 * x * (1.0 + jnp.tanh(np.sqrt(2.0 / np.pi).astype(np.float32) * (x + 0.044715 * (x * x * x))))


def _rmsnorm(x, g):
    return x * lax.rsqrt(jnp.mean(x * x, -1, keepdims=True) + EPS) * g


def _layernorm(x, g, b):
    xc = x - jnp.mean(x, -1, keepdims=True)
    return xc * lax.rsqrt(jnp.mean(xc * xc, -1, keepdims=True) + EPS) * g + b


def _modulate(x, g, shift, scale):
    return _rmsnorm(x, g) * (1.0 + scale) + shift


def _dot(a, b):
    return jnp.dot(a, b, preferred_element_type=F32)


def _dot_nt(a, b):
    return lax.dot_general(a, b, (((1,), (1,)), ((), ())), preferred_element_type=F32)


def _lane_iota(shape):
    return lax.broadcasted_iota(jnp.int32, shape, len(shape) - 1)


def _masked_softmax(s, mask):
    s = jnp.where(mask, s, NEG)
    m = jnp.max(s, -1, keepdims=True)
    p = jnp.where(mask, jnp.exp(s - m), 0.0)
    return p / jnp.maximum(jnp.sum(p, -1, keepdims=True), 1e-30)


def _rope_pair(xs, cos, sin_signed):
    half = HEAD_DIM // 2
    first = (_lane_iota(xs.shape) & (HEAD_DIM - 1)) < half
    swapped = jnp.where(first, pltpu.roll(xs, LANES - half, 1), pltpu.roll(xs, half, 1))
    return xs * cos + swapped * sin_signed


def _select_blocks(imp, pos, n_top):
    lane = _lane_iota(imp.shape)
    j = lane & (SEL_LANES - 1)
    cur = pos >> SEL_SHIFT
    valid = j * SEL_BLOCK <= pos
    forced = (j == 0) | (j == cur) | (j == cur - 1)
    score = jnp.where(valid, jnp.where(forced, FORCED, imp), NEG)
    rank = jnp.zeros(imp.shape, F32)
    for d in range(1, SEL_LANES):
        nonwrap = j >= d
        rot = jnp.where(nonwrap, pltpu.roll(score, d, 1), pltpu.roll(score, (d - SEL_LANES) % LANES, 1))
        ahead = (rot > score) | ((rot == score) & nonwrap)
        rank = rank + jnp.where(ahead, 1.0, 0.0)
    return jnp.where(rank < n_top, 1.0, 0.0)


def _select_blocks_t(imp_t, pos, n_top):
    row = lax.broadcasted_iota(jnp.int32, (LANES, 1), 0)
    j = row & (SEL_LANES - 1)
    cur = pos >> SEL_SHIFT
    valid = j * SEL_BLOCK <= pos
    forced = (j == 0) | (j == cur) | (j == cur - 1)
    score = jnp.where(valid, jnp.where(forced, FORCED, imp_t), NEG)
    jj = lax.broadcasted_iota(jnp.int32, (SEL_LANES, 1), 0)
    out = []
    for k in range(N_KV):
        sc = score[k * SEL_LANES:(k + 1) * SEL_LANES]
        rank = jnp.zeros(sc.shape, F32)
        for i in range(SEL_LANES):
            other = sc[i:i + 1]
            tie_first = jnp.where(jj > i, 1.0, 0.0)
            rank = rank + jnp.where(other > sc, 1.0, 0.0) + jnp.where(other == sc, tie_first, 0.0)
        out.append(jnp.where(rank < n_top, 1.0, 0.0))
    return jnp.concatenate(out, axis=0)


def _ada_body(c_ref, w_ref, b_ref, o_ref):
    a = _silu(c_ref[...])
    o_ref[0, 0] = jnp.dot(a, w_ref[0], preferred_element_type=F32, precision=HIGHEST) + b_ref[0, 0]


def _ada_all(c_all, ada_w, ada_b):
    depth = ada_w.shape[0]
    rows = c_all.shape[0]
    return pl.pallas_call(
        _ada_body,
        out_shape=jax.ShapeDtypeStruct((depth, 6, rows, D_MODEL), F32),
        grid=(depth, 6),
        in_specs=[
            _const_spec((rows, D_MODEL)),
            pl.BlockSpec((1, D_MODEL, D_MODEL), lambda l, j: (l, 0, j)),
            pl.BlockSpec((1, 1, 1, D_MODEL), lambda l, j: (l, j, 0, 0)),
        ],
        out_specs=pl.BlockSpec((1, 1, rows, D_MODEL), lambda l, j: (l, j, 0, 0)),
        compiler_params=_cparams("arbitrary", "arbitrary"),
        name="ada_params",
    )(c_all, ada_w, ada_b.reshape(depth, 6, 1, D_MODEL))


def _post_body(x_ref, y_ref, mod_ref, gffn_ref, wo_ref, w1_ref, w2_ref, gfin_ref, o_ref, *, final, ff_chunk):
    x = x_ref[0]
    x = x + mod_ref[2, 0] * _dot(y_ref[0], wo_ref[...])
    h = _modulate(x, gffn_ref[...], mod_ref[3, 0], mod_ref[4, 0]).astype(BF16)
    acc = jnp.zeros(x.shape, F32)
    for c in range(D_FF // ff_chunk):
        t = _dot(h, w1_ref[:, c * ff_chunk:(c + 1) * ff_chunk])
        t = jnp.square(jnp.maximum(t, 0.0)).astype(BF16)
        acc = acc + _dot(t, w2_ref[c * ff_chunk:(c + 1) * ff_chunk, :])
    x = x + mod_ref[5, 0] * acc
    if final:
        x = _rmsnorm(x, gfin_ref[...])
    o_ref[0] = x


def _post(x, y, mods, g_ffn, w_out, w1, w2, g_fin, *, final, tm):
    bn, ln, _ = x.shape
    d_in = y.shape[-1]
    r = mods.shape[2]
    tm = min(tm, ln)
    mod_rows = 1 if r == 1 else tm
    mod_map = (lambda b, t: (0, b, 0, 0)) if r == 1 else (lambda b, t: (0, b, t, 0))
    return pl.pallas_call(
        functools.partial(_post_body, final=final, ff_chunk=1024),
        out_shape=jax.ShapeDtypeStruct(x.shape, F32),
        grid=(bn, ln // tm),
        in_specs=[
            pl.BlockSpec((1, tm, D_MODEL), lambda b, t: (b, t, 0)),
            pl.BlockSpec((1, tm, d_in), lambda b, t: (b, t, 0)),
            pl.BlockSpec((6, 1, mod_rows, D_MODEL), mod_map),
            _const_spec((1, D_MODEL)),
            _const_spec((d_in, D_MODEL)),
            _const_spec((D_MODEL, D_FF)),
            _const_spec((D_FF, D_MODEL)),
            _const_spec((1, D_MODEL)),
        ],
        out_specs=pl.BlockSpec((1, tm, D_MODEL), lambda b, t: (b, t, 0)),
        compiler_params=_cparams("arbitrary", "arbitrary"),
        name="post_mlp",
    )(x, y, mods, g_ffn, w_out, w1, w2, g_fin)


def _even_prompt_body(x_ref, mod_ref, gmix_ref, win_ref, cw_ref, cb_ref, clg_ref, clb_ref, slg_ref, slb_ref,
                      sw_ref, sbt_ref, y_ref, conv_ref, chv_ref, ext_ref, sh_ref, *, tm, conv_rows):
    t = pl.program_id(1)
    h = _modulate(x_ref[0], gmix_ref[...], mod_ref[0, 0], mod_ref[1, 0]).astype(BF16)
    z = _dot(h, win_ref[...])
    a = z[:, :D_A] * _sigmoid(z[:, D_A:2 * D_A])
    u = _gelu(z[:, 2 * D_A:2 * D_A + D_B])
    v = _layernorm(_gelu(z[:, 2 * D_A + D_B:]), slg_ref[...], slb_ref[...])

    @pl.when(t == 0)
    def _():
        ext_ref[0:CONV_HALO, :] = jnp.zeros((CONV_HALO, D_A), F32)

    ext_ref[CONV_HALO:CONV_HALO + tm, :] = a
    first = CONV_HALO - (CONV_WIDTH - 1)
    span = sh_ref.shape[1]
    for sft in range(1, SUBLANES):
        sh_ref[sft - 1] = ext_ref[sft:sft + span, :]
    pieces = []
    for c in range(tm // conv_rows):
        acc = jnp.zeros((conv_rows, D_A), F32) + cb_ref[...]
        for w in range(CONV_WIDTH):
            sft = (first + w) % SUBLANES
            r0 = first + w - sft + c * conv_rows
            src = ext_ref[r0:r0 + conv_rows, :] if sft == 0 else sh_ref[sft - 1, r0:r0 + conv_rows, :]
            acc = acc + src * cw_ref[w:w + 1, :]
        pieces.append(acc)
    a_conv = jnp.concatenate(pieces, axis=0)
    conv_ref[0] = ext_ref[tm + first:tm + CONV_HALO, :]
    ext_ref[0:CONV_HALO, :] = ext_ref[tm:tm + CONV_HALO, :]
    a_out = _silu(_layernorm(a_conv, clg_ref[...], clb_ref[...]))

    row = lax.broadcasted_iota(jnp.int32, (CHUNK, CHUNK), 0)
    col = lax.broadcasted_iota(jnp.int32, (CHUNK, CHUNK), 1)
    wm = [jnp.where(col <= row, sw_ref[g], 0.0).astype(BF16) for g in range(B_GROUPS)]
    vb = v.astype(BF16)
    gate_rows = []
    for c in range(tm // CHUNK):
        cols = []
        for g in range(B_GROUPS):
            vg = vb[c * CHUNK:(c + 1) * CHUNK, g * B_GROUP_DIM:(g + 1) * B_GROUP_DIM]
            cols.append(_dot(wm[g], vg) + sbt_ref[:, g:g + 1])
        gate_rows.append(jnp.concatenate(cols, axis=1))
    b_out = u * jnp.concatenate(gate_rows, axis=0)

    y_ref[0, :, :D_A] = a_out.astype(BF16)
    y_ref[0, :, D_A:] = b_out.astype(BF16)
    chv_ref[0] = v[tm - CHUNK:, :]


def _even_prompt(x, mods, g_mix, w_in, conv_w, conv_b, cln_g, cln_b, sln_g, sln_b, sgu_w, sgu_bt, *, tm):
    bn, ln, _ = x.shape
    assert ln % tm == 0 and tm % CHUNK == 0 and ln >= CONV_HALO
    return pl.pallas_call(
        functools.partial(_even_prompt_body, tm=tm, conv_rows=64),
        out_shape=(
            jax.ShapeDtypeStruct((bn, ln, D_MODEL), BF16),
            jax.ShapeDtypeStruct((bn, CONV_WIDTH - 1, D_A), F32),
            jax.ShapeDtypeStruct((bn, CHUNK, D_B), F32),
        ),
        grid=(bn, ln // tm),
        in_specs=[
            pl.BlockSpec((1, tm, D_MODEL), lambda b, t: (b, t, 0)),
            pl.BlockSpec((6, 1, 1, D_MODEL), lambda b, t: (0, b, 0, 0)),
            _const_spec((1, D_MODEL)),
            _const_spec((D_MODEL, 2 * D_A + 2 * D_B)),
            _const_spec((CONV_WIDTH, D_A)),
            _const_spec((1, D_A)), _const_spec((1, D_A)), _const_spec((1, D_A)),
            _const_spec((1, D_B)), _const_spec((1, D_B)),
            _const_spec((B_GROUPS, CHUNK, CHUNK)),
            _const_spec((CHUNK, B_GROUPS)),
        ],
        out_specs=(
            pl.BlockSpec((1, tm, D_MODEL), lambda b, t: (b, t, 0)),
            pl.BlockSpec((1, CONV_WIDTH - 1, D_A), lambda b, t: (b, 0, 0)),
            pl.BlockSpec((1, CHUNK, D_B), lambda b, t: (b, 0, 0)),
        ),
        scratch_shapes=[pltpu.VMEM((tm + CONV_HALO, D_A), F32),
                        pltpu.VMEM((SUBLANES - 1, tm + CONV_HALO - SUBLANES, D_A), F32)],
        compiler_params=_cparams("arbitrary", "arbitrary"),
        name="even_prompt",
    )(x, mods, g_mix, w_in, conv_w, conv_b, cln_g, cln_b, sln_g, sln_b, sgu_w, sgu_bt)


def _even_sample_body(x_ref, mod_ref, gmix_ref, win_ref, st_ref, cw_ref, cb_ref, clg_ref, clb_ref, slg_ref,
                      slb_ref, w00_ref, b0_ref, y_ref, a_ref, v_ref):
    h = _modulate(x_ref[...], gmix_ref[...], mod_ref[0], mod_ref[1]).astype(BF16)
    z = _dot(h, win_ref[...])
    a = z[:, :D_A] * _sigmoid(z[:, D_A:2 * D_A])
    u = _gelu(z[:, 2 * D_A:2 * D_A + D_B])
    v = _layernorm(_gelu(z[:, 2 * D_A + D_B:]), slg_ref[...], slb_ref[...])
    acc = a * cw_ref[CONV_WIDTH - 1:CONV_WIDTH, :] + cb_ref[...]
    for w in range(CONV_WIDTH - 1):
        acc = acc + st_ref[w] * cw_ref[w:w + 1, :]
    a_out = _silu(_layernorm(acc, clg_ref[...], clb_ref[...]))
    b_out = u * (w00_ref[...] * v + b0_ref[...])
    y_ref[:, :D_A] = a_out.astype(BF16)
    y_ref[:, D_A:] = b_out.astype(BF16)
    a_ref[...] = a
    v_ref[...] = v


def _even_sample(x, mods, g_mix, w_in, state_t, conv_w, conv_b, cln_g, cln_b, sln_g, sln_b, w00, b0):
    sn = x.shape[0]
    shapes = [a.shape for a in (x, mods, g_mix, w_in, state_t, conv_w, conv_b, cln_g, cln_b, sln_g, sln_b, w00, b0)]
    return pl.pallas_call(
        _even_sample_body,
        out_shape=(
            jax.ShapeDtypeStruct((sn, D_MODEL), BF16),
            jax.ShapeDtypeStruct((sn, D_A), F32),
            jax.ShapeDtypeStruct((sn, D_B), F32),
        ),
        grid=(1,),
        in_specs=[_const_spec(s) for s in shapes],
        out_specs=(_const_spec((sn, D_MODEL)), _const_spec((sn, D_A)), _const_spec((sn, D_B))),
        compiler_params=_cparams("arbitrary"),
        name="even_sample",
    )(x, mods, g_mix, w_in, state_t, conv_w, conv_b, cln_g, cln_b, sln_g, sln_b, w00, b0)


_Q_LOW = [h for h in range(N_HEADS) if (h // GROUP) % 2 == 0]
_Q_HIGH = [h for h in range(N_HEADS) if (h // GROUP) % 2 == 1]
Q_HEAD_ORDER = [h for pair in zip(_Q_LOW, _Q_HIGH) for h in pair]


def _proj_body(x_ref, mod_ref, gmix_ref, w_ref, cos_ref, sin_ref, *out_refs, decode):
    if decode:
        qs_ref, qrs_ref, rows_ref, cols_ref, g_ref = out_refs
    else:
        qs_ref, qrs_ref, c0_ref, c1_ref, c2_ref, c3_ref, rowsb_ref, cols_ref, colsb_ref, g_ref = out_refs
    tm = x_ref.shape[1]
    h = _modulate(x_ref[0], gmix_ref[...], mod_ref[0, 0], mod_ref[1, 0]).astype(BF16)
    z = _dot(h, w_ref[...])
    cos = cos_ref[...]
    sin = sin_ref[...]
    q0 = N_HEADS * HEAD_DIM
    low = _lane_iota((tm, LANES)) < HEAD_DIM
    for c in range(q0 // LANES):
        xs = z[:, c * LANES:(c + 1) * LANES] * SCALE
        xr = _rope_pair(xs, cos, sin)
        for half, keep in ((0, low), (1, jnp.logical_not(low))):
            hd = Q_HEAD_ORDER[2 * c + half]
            qs_ref[0, hd] = jnp.where(keep, xs, 0.0).astype(BF16)
            qrs_ref[0, hd] = jnp.where(keep, xr, 0.0).astype(BF16)

    def kv(i):
        return z[:, q0 + i * KV_DIM:q0 + (i + 1) * KV_DIM]

    def kv_rot(i):
        base = q0 + i * KV_DIM
        return jnp.concatenate(
            [_rope_pair(z[:, base + c * LANES:base + (c + 1) * LANES], cos, sin) for c in range(N_PAIR)], axis=1)

    rows = jnp.concatenate([kv(0), kv(1), kv_rot(2), kv(3), kv_rot(4), kv(5)], axis=1)
    cols = rows.T
    cols_ref[0] = cols
    if decode:
        rows_ref[0] = rows[:, 2 * KV_DIM:]
    else:
        for i, ref in enumerate((c0_ref, c1_ref, c2_ref, c3_ref)):
            ref[0] = rows[:, i * LANES:(i + 1) * LANES]
        rowsb_ref[0] = rows[:, 2 * KV_DIM:].astype(BF16)
        colsb_ref[0] = cols[2 * KV_DIM:].astype(BF16)
    g_ref[0] = _sigmoid(z[:, q0 + 6 * KV_DIM:])


def _nsa_project(x, mods, g_mix, w_in, cos, sin, *, tm, decode):
    bn, ln, _ = x.shape
    r = mods.shape[2]
    tm = min(tm, ln)
    mod_rows = 1 if r == 1 else tm
    mod_map = (lambda b, t: (0, b, 0, 0)) if r == 1 else (lambda b, t: (0, b, t, 0))
    tok = lambda w, dt: (jax.ShapeDtypeStruct((bn, ln, w), dt), pl.BlockSpec((1, tm, w), lambda b, t: (b, t, 0)))
    col = lambda w, dt: (jax.ShapeDtypeStruct((bn, w, ln), dt), pl.BlockSpec((1, w, tm), lambda b, t: (b, 0, t)))
    head = (jax.ShapeDtypeStruct((bn, N_HEADS, ln, LANES), BF16),
            pl.BlockSpec((1, N_HEADS, tm, LANES), lambda b, t: (b, 0, t, 0)))
    if decode:
        outs = [head, head, tok(4 * KV_DIM, F32), col(6 * KV_DIM, F32), tok(LANES, F32)]
    else:
        outs = ([head, head] + [tok(LANES, F32)] * 4
                + [tok(4 * KV_DIM, BF16), col(6 * KV_DIM, F32), col(4 * KV_DIM, BF16), tok(LANES, F32)])
    return pl.pallas_call(
        functools.partial(_proj_body, decode=decode),
        out_shape=tuple(o[0] for o in outs),
        grid=(bn, ln // tm),
        in_specs=[
            pl.BlockSpec((1, tm, D_MODEL), lambda b, t: (b, t, 0)),
            pl.BlockSpec((6, 1, mod_rows, D_MODEL), mod_map),
            _const_spec((1, D_MODEL)),
            _const_spec((D_MODEL, ODD_IN_PAD)),
            pl.BlockSpec((tm, LANES), lambda b, t: (t, 0)),
            pl.BlockSpec((tm, LANES), lambda b, t: (t, 0)),
        ],
        out_specs=tuple(o[1] for o in outs),
        compiler_params=_cparams("arbitrary", "arbitrary"),
        name="nsa_project",
    )(x, mods, g_mix, w_in, cos, sin)


def _compress_rows(load_rows, pe2_ref, w1_ref, w2e_ref, w2o_ref):
    n_chunk = 128
    lane = _lane_iota((n_chunk, LANES))
    low = lane < HEAD_DIM
    heads = [[] for _ in range(N_KV)]
    for p in range(CMP_STRIDE // 2):
        for m in range(N_PAIR):
            lo = load_rows(2 * p, m)
            hi = load_rows(2 * p + 1, m)
            heads[2 * m].append(jnp.where(low, lo, pltpu.roll(hi, HEAD_DIM, 1)))
            heads[2 * m + 1].append(jnp.where(low, pltpu.roll(lo, HEAD_DIM, 1), hi))
    x_all = jnp.concatenate([jnp.concatenate(hp, axis=1) for hp in heads], axis=0).astype(BF16)
    ab = _dot(x_all, w1_ref[...])
    pe = _dot(pe2_ref[...], w1_ref[...])
    hid0 = pe[0:1, :CMP_HIDDEN] + pe[1:2, CMP_HIDDEN:]
    hid = []
    for k in range(N_KV):
        a_part = ab[k * n_chunk:(k + 1) * n_chunk, :CMP_HIDDEN]
        b_part = ab[k * n_chunk:(k + 1) * n_chunk, CMP_HIDDEN:]
        nxt = pltpu.roll(b_part, n_chunk - 1, 0)
        hid.append(_gelu(a_part + nxt + hid0).astype(BF16))
    out = []
    for m in range(N_PAIR):
        out.append(_dot(hid[2 * m], w2e_ref[...]) + _dot(hid[2 * m + 1], w2o_ref[...]))
    return jnp.concatenate(out, axis=1)


def _compress_prompt_body(c0_ref, c1_ref, c2_ref, c3_ref, pek_ref, w1k_ref, w2ke_ref, w2ko_ref,
                          pev_ref, w1v_ref, w2ve_ref, w2vo_ref, kc_ref, vc_ref):
    slabs = (c0_ref, c1_ref, c2_ref, c3_ref)
    n_chunk = c0_ref.shape[1] // CMP_STRIDE
    assert n_chunk == 128

    def rows(first):
        return lambda j, m: slabs[first + m][0, pl.ds(j, n_chunk, stride=CMP_STRIDE), :]

    kc_ref[0] = _compress_rows(rows(0), pek_ref, w1k_ref, w2ke_ref, w2ko_ref).astype(BF16)
    vc_ref[0] = _compress_rows(rows(N_PAIR), pev_ref, w1v_ref, w2ve_ref, w2vo_ref).astype(BF16)


def _compress_prompt(slabs, cw_k, cw_v):
    bn, ln, _ = slabs[0].shape
    wspecs = [_const_spec(a.shape) for a in cw_k + cw_v]
    seq = pl.BlockSpec((1, ln, LANES), lambda b: (b, 0, 0))
    out = pl.BlockSpec((1, 128, KV_DIM), lambda b: (b, 0, 0))
    return pl.pallas_call(
        _compress_prompt_body,
        out_shape=(jax.ShapeDtypeStruct((bn, 128, KV_DIM), BF16),) * 2,
        grid=(bn,),
        in_specs=[seq] * 4 + wspecs,
        out_specs=(out, out),
        compiler_params=_cparams("arbitrary"),
        name="compress_prompt",
    )(*slabs, *cw_k, *cw_v)


ONES_ROWS = 16


def _flash_step(carry, s, v1):
    m_i, acc = carry
    m_new = jnp.maximum(m_i, jnp.max(s, 0, keepdims=True))
    alpha = jnp.exp(m_i - m_new)
    p = jnp.exp(s - m_new)
    return m_new, alpha * acc + _dot(v1, p.astype(BF16))


def _attn_prompt_body(qs_ref, qrs_ref, g_ref, kc_ref, vc_ref, krow_ref, kvt_ref, ovt_ref, nege_ref, o_ref, *, t):
    qi = pl.program_id(1)
    q0 = pl.multiple_of(qi * t, t)
    rows4 = GROUP * t
    pos_t = q0 + (_lane_iota((1, rows4)) & (t - 1))
    gates_t = g_ref[0].T

    def q_t(ref, k):
        return ref[0, GROUP * k:GROUP * (k + 1)].reshape(rows4, LANES).astype(F32).T.astype(BF16)

    def per_token(x):
        return jnp.concatenate([x] * GROUP, axis=1)

    key_n = lax.broadcasted_iota(jnp.int32, (LANES, 1), 0)
    cmask = (key_n < LANES - 1) & (key_n * CMP_STRIDE + (CMP_BLOCK - 1) <= pos_t)
    o_cmp = []
    imp_t = jnp.zeros((LANES, t), F32)
    for k in range(N_KV):
        m = k // 2
        s = jnp.where(cmask, _dot(kc_ref[0, :, m * LANES:(m + 1) * LANES], q_t(qs_ref, k)), NEG)
        p = jnp.where(cmask, jnp.exp(s - jnp.max(s, 0, keepdims=True)), 0.0)
        p = p / jnp.maximum(jnp.sum(p, 0, keepdims=True), 1e-30)
        o_cmp.append(_dot(vc_ref[0, :, m * LANES:(m + 1) * LANES].astype(F32).T.astype(BF16), p.astype(BF16)))
        psum = p[:, 0:t] + p[:, t:2 * t] + p[:, 2 * t:3 * t] + p[:, 3 * t:4 * t]
        imp_t = imp_t + jnp.dot(ovt_ref[k], psum, preferred_element_type=F32, precision=HIGHEST)
    unsel_t = per_token((1.0 - _select_blocks_t(imp_t, pos_t[:, :t], N_SEL)).astype(BF16))
    ones = jnp.ones((ONES_ROWS, t), BF16)

    kr = lax.broadcasted_iota(jnp.int32, (t, t), 0)
    qc = lax.broadcasted_iota(jnp.int32, (t, t), 1)
    bias_diag = per_token(jnp.where(kr <= qc, 0.0, MASK_BIAS))
    bias_edge = per_token(jnp.where(kr >= qc, 0.0, MASK_BIAS))

    def k_tile(stream, m, kt):
        k0 = pl.multiple_of(kt * t, t)
        base = stream * KV_DIM + m * LANES
        return krow_ref[0, pl.ds(k0, t), base:base + LANES]

    def v_ones(stream, m, kt):
        k0 = pl.multiple_of(kt * t, t)
        base = stream * KV_DIM + m * LANES
        return jnp.concatenate([kvt_ref[0, base:base + LANES, pl.ds(k0, t)], ones], axis=0)

    def init():
        return jnp.full((1, rows4), NEG, F32), jnp.zeros((LANES + ONES_ROWS, rows4), F32)

    def finish(carry):
        _, acc = carry
        return acc[:LANES] / jnp.maximum(acc[LANES:LANES + 1], 1e-30)

    heads = [None] * N_HEADS
    for k in range(N_KV):
        m = k // 2
        dims = slice((k % 2) * HEAD_DIM, (k % 2 + 1) * HEAD_DIM)
        qr_t = q_t(qrs_ref, k)

        qu_t = jnp.concatenate([qr_t, unsel_t], axis=0)

        def sel_scores(kt, m=m, k=k, qu_t=qu_t):
            k0 = pl.multiple_of(kt * t, t)
            return _dot(jnp.concatenate([k_tile(0, m, kt), nege_ref[k, pl.ds(k0, t), :]], axis=1), qu_t)

        def sel_step(kt, cr, m=m, sel_scores=sel_scores):
            return _flash_step(cr, sel_scores(kt), v_ones(1, m, kt))

        def sel_pair(i, cr, sel_step=sel_step):
            return sel_step(2 * i + 1, sel_step(2 * i, cr))

        carry = lax.fori_loop(0, qi // 2, sel_pair, init())
        carry = lax.cond(qi % 2 == 1, lambda cr: sel_step(qi - 1, cr), lambda cr: cr, carry)
        carry = _flash_step(carry, sel_scores(qi) + bias_diag, v_ones(1, m, qi))
        o_sel = finish(carry)

        carry = init()
        for back, bias in ((2, bias_edge), (1, None)):
            def visit(cr, back=back, bias=bias, m=m, qr_t=qr_t):
                s = _dot(k_tile(2, m, qi - back), qr_t)
                return _flash_step(cr, s if bias is None else s + bias, v_ones(3, m, qi - back))
            carry = lax.cond(qi >= back, visit, lambda cr: cr, carry)
        carry = _flash_step(carry, _dot(k_tile(2, m, qi), qr_t) + bias_diag, v_ones(3, m, qi))
        o_win = finish(carry)

        for g in range(GROUP):
            h = GROUP * k + g
            cols = slice(g * t, (g + 1) * t)
            heads[h] = (gates_t[3 * h:3 * h + 1] * o_cmp[k][dims, cols] + gates_t[3 * h + 1:3 * h + 2] * o_sel[dims, cols]
                        + gates_t[3 * h + 2:3 * h + 3] * o_win[dims, cols])
    o_ref[0] = jnp.concatenate(heads, axis=0).T.astype(BF16)


def _attn_prompt(qs, qrs, gates, kc, vc, krow, kvt, ov_t, nege, *, t):
    bn, _, ln, _ = qs.shape
    assert ln % t == 0 and ln <= SEL_LANES * SEL_BLOCK and WINDOW == 2 * t and t & (t - 1) == 0
    head = pl.BlockSpec((1, N_HEADS, t, LANES), lambda b, i: (b, 0, i, 0))
    cmp_spec = pl.BlockSpec((1, 128, KV_DIM), lambda b, i: (b, 0, 0))
    return pl.pallas_call(
        functools.partial(_attn_prompt_body, t=t),
        out_shape=jax.ShapeDtypeStruct((bn, ln, D_MODEL), BF16),
        grid=(bn, ln // t),
        in_specs=[head, head, pl.BlockSpec((1, t, LANES), lambda b, i: (b, i, 0)), cmp_spec, cmp_spec,
                  pl.BlockSpec((1, ln, 4 * KV_DIM), lambda b, i: (b, 0, 0)),
                  pl.BlockSpec((1, 4 * KV_DIM, ln), lambda b, i: (b, 0, 0)),
                  _const_spec(ov_t.shape), _const_spec(nege.shape)],
        out_specs=pl.BlockSpec((1, t, D_MODEL), lambda b, i: (b, i, 0)),
        compiler_params=_cparams("arbitrary", "arbitrary"),
        name="attn_prompt",
    )(qs, qrs, gates, kc, vc, krow, kvt, ov_t, nege)


def _attn_sample_body(pt_ref, qs_ref, qrs_ref, g_ref, new_ref, wkc_ref, wvc_ref, wks_ref, wvs_ref, *rest, n_pages):
    del pt_ref
    pools = [rest[i * n_pages:(i + 1) * n_pages] for i in range(4)]
    (pek_ref, w1k_ref, w2ke_ref, w2ko_ref, pev_ref, w1v_ref, w2ve_ref, w2vo_ref, ov_ref, expand_ref,
     o_ref, wko_ref, wvo_ref, x_ref) = rest[4 * n_pages:]
    past = n_pages * PAGE_SIZE

    def compress(pages, *weights):
        pitch = CMP_STRIDE + 1
        for i, pg in enumerate(pages):
            for m in range(N_PAIR):
                rows_t = pg[0, 0, m * LANES:(m + 1) * LANES, :].T
                for c in range(PAGE_SIZE // CMP_STRIDE):
                    r0 = (i * (PAGE_SIZE // CMP_STRIDE) + c) * pitch
                    x_ref[m, r0:r0 + CMP_STRIDE, :] = rows_t[c * CMP_STRIDE:(c + 1) * CMP_STRIDE]
        load = lambda j, m: x_ref[m, pl.ds(j, past // CMP_STRIDE, stride=pitch), :]
        return _compress_rows(load, *weights).astype(BF16)

    kc = compress(pools[0], pek_ref, w1k_ref, w2ke_ref, w2ko_ref)
    vc = compress(pools[1], pev_ref, w1v_ref, w2ve_ref, w2vo_ref)

    lane = _lane_iota((SUBLANES, LANES))
    upper_rows = lax.broadcasted_iota(jnp.int32, (SUBLANES, LANES), 0) >= GROUP
    gates = g_ref[0]
    pos = jnp.full((SUBLANES, 1), past, jnp.int32)

    cmask = lane < LANES - 1
    o_cmp = []
    imp = jnp.zeros((SUBLANES, LANES), F32)
    for m in range(N_PAIR):
        qs = qs_ref[0, 2 * GROUP * m:2 * GROUP * (m + 1), :]
        p = _masked_softmax(_dot_nt(qs, kc[:, m * LANES:(m + 1) * LANES]), cmask)
        o_cmp.append(_dot(p.astype(BF16), vc[:, m * LANES:(m + 1) * LANES]))
        for kk in range(2):
            psum = jnp.sum(p[kk * GROUP:(kk + 1) * GROUP], axis=0, keepdims=True)
            imp = imp + jnp.dot(jnp.broadcast_to(psum, (SUBLANES, LANES)), ov_ref[2 * m + kk],
                                preferred_element_type=F32, precision=HIGHEST)
    sel = _select_blocks(imp, pos, N_SEL - 1)
    seg = lane >> SEL_LANES_SHIFT
    expand = expand_ref[...]

    def new_row(stream, m):
        base = stream * KV_DIM + m * LANES
        return new_ref[0, :, base:base + LANES]

    for m in range(N_PAIR):
        rows = slice(m * LANES, (m + 1) * LANES)
        qrs = qrs_ref[0, 2 * GROUP * m:2 * GROUP * (m + 1), :]
        qrf = qrs.astype(F32)
        want = jnp.where(upper_rows, 2 * m + 1, 2 * m)
        sel_rows = jnp.where(seg == want, sel, 0.0).astype(BF16)
        ok = _dot(sel_rows, expand) > 0.5
        s = jnp.concatenate([_dot(qrs, pg[0, 0, rows, :].astype(BF16)) for pg in pools[2]], axis=1)
        s_new = jnp.sum(qrf * new_row(0, m), -1, keepdims=True)
        s = jnp.where(ok, s, NEG)
        mx = jnp.maximum(jnp.max(s, -1, keepdims=True), s_new)
        p = jnp.where(ok, jnp.exp(s - mx), 0.0)
        p_new = jnp.exp(s_new - mx)
        denom = jnp.maximum(jnp.sum(p, -1, keepdims=True) + p_new, 1e-30)
        pb = p.astype(BF16)
        acc = p_new * new_row(1, m)
        for i, pg in enumerate(pools[3]):
            acc = acc + _dot_nt(pb[:, i * PAGE_SIZE:(i + 1) * PAGE_SIZE], pg[0, 0, rows, :].astype(BF16))
        o_sel = acc / denom
        sw = _dot(qrs, wks_ref[0, 0, rows, :].astype(BF16))
        sw_new = jnp.sum(qrf * new_row(2, m), -1, keepdims=True)
        mw = jnp.maximum(jnp.max(sw, -1, keepdims=True), sw_new)
        pw = jnp.exp(sw - mw)
        pw_new = jnp.exp(sw_new - mw)
        dw = jnp.maximum(jnp.sum(pw, -1, keepdims=True) + pw_new, 1e-30)
        o_win = (_dot_nt(pw.astype(BF16), wvs_ref[0, 0, rows, :].astype(BF16)) + pw_new * new_row(3, m)) / dw
        for kk in range(2):
            for c in range(GROUP // 2):
                halves = []
                for gg in range(2):
                    g = 2 * c + gg
                    h = GROUP * (2 * m + kk) + g
                    r = kk * GROUP + g
                    o = (gates[:, 3 * h:3 * h + 1] * o_cmp[m][r:r + 1] + gates[:, 3 * h + 1:3 * h + 2] * o_sel[r:r + 1]
                         + gates[:, 3 * h + 2:3 * h + 3] * o_win[r:r + 1])
                    if gg != kk:
                        o = pltpu.roll(jnp.broadcast_to(o, (SUBLANES, LANES)), HEAD_DIM, 1)[0:1]
                    halves.append(o)
                col = (GROUP * (2 * m + kk)) // 2 + c
                o_ref[0, :, col * LANES:(col + 1) * LANES] = jnp.where(
                    _lane_iota((1, LANES)) < HEAD_DIM, halves[0], halves[1]).astype(BF16)

    nbuf = wks_ref.shape[-1]
    last = _lane_iota((KV_DIM, nbuf)) == nbuf - 1
    wko_ref[0] = jnp.where(last, wkc_ref[0], pltpu.roll(wks_ref[0, 0], nbuf - 1, 1))
    wvo_ref[0] = jnp.where(last, wvc_ref[0], pltpu.roll(wvs_ref[0, 0], nbuf - 1, 1))


def _attn_sample(page_table, layer, qs, qrs, gates, new_rows, wk_col, wv_col, win_k, win_v,
                 pool_ck, pool_cv, pool_sk, pool_sv, cw_k, cw_v, ov):
    sn = qs.shape[0]
    n_pages = page_table.shape[1]
    nbuf = win_k.shape[-1]
    past = n_pages * PAGE_SIZE
    assert nbuf == WINDOW and past == SEL_LANES * SEL_BLOCK
    per_seq = lambda *blk: pl.BlockSpec((1,) + blk, lambda s, pt: (s,) + (0,) * len(blk))
    page_map = lambda s, pt, p: (layer, pt[s, p], 0, 0)
    page_specs = [pl.BlockSpec((1, 1, KV_DIM, PAGE_SIZE), functools.partial(page_map, p=p)) for p in range(n_pages)]
    win_spec = pl.BlockSpec((1, 1, KV_DIM, nbuf), lambda s, pt: (layer, s, 0, 0))
    expand = (np.arange(LANES)[:, None] % SEL_LANES) == (np.arange(past)[None, :] // SEL_BLOCK)
    weights = cw_k + cw_v + (ov, jnp.asarray(expand, dtype=BF16))
    grid_spec = pltpu.PrefetchScalarGridSpec(
        num_scalar_prefetch=1,
        grid=(sn,),
        in_specs=[per_seq(N_HEADS, LANES), per_seq(N_HEADS, LANES), per_seq(1, LANES), per_seq(1, 4 * KV_DIM),
                  per_seq(KV_DIM, 1), per_seq(KV_DIM, 1), win_spec, win_spec]
        + page_specs * 4
        + [pl.BlockSpec(w.shape, functools.partial(lambda s, pt, n: (0,) * n, n=w.ndim)) for w in weights],
        out_specs=(per_seq(1, D_MODEL), per_seq(KV_DIM, nbuf), per_seq(KV_DIM, nbuf)),
        scratch_shapes=[pltpu.VMEM((N_PAIR, past // CMP_STRIDE * (CMP_STRIDE + 1) + SUBLANES - 1, LANES), F32)],
    )
    pools = [pool_ck] * n_pages + [pool_cv] * n_pages + [pool_sk] * n_pages + [pool_sv] * n_pages
    return pl.pallas_call(
        functools.partial(_attn_sample_body, n_pages=n_pages),
        out_shape=(jax.ShapeDtypeStruct((sn, 1, D_MODEL), BF16),
                   jax.ShapeDtypeStruct((sn, KV_DIM, nbuf), F32),
                   jax.ShapeDtypeStruct((sn, KV_DIM, nbuf), F32)),
        grid_spec=grid_spec,
        compiler_params=_cparams("arbitrary"),
        name="attn_sample",
    )(page_table, qs, qrs, gates, new_rows, wk_col, wv_col, win_k, win_v, *pools, *weights)


def _rope_tables(pos):
    half = HEAD_DIM // 2
    inv = jnp.power(jnp.float32(ROPE_THETA), -jnp.arange(half, dtype=F32) * (2.0 / HEAD_DIM))
    ang = pos.astype(F32)[:, None] * inv[None, :]
    cos, sin = jnp.cos(ang), jnp.sin(ang)
    return jnp.concatenate([cos, cos, cos, cos], -1), jnp.concatenate([-sin, sin, -sin, sin], -1)


def _overlap_tables():
    n_cmp, n_sel = LANES - 1, SEL_LANES
    i = np.arange(n_cmp)[:, None] * CMP_STRIDE
    j = np.arange(n_sel)[None, :] * SEL_BLOCK
    ov = ((i < j + SEL_BLOCK) & (i + CMP_BLOCK > j)).astype(np.float32)
    out = np.zeros((N_KV, LANES, LANES), np.float32)
    for k in range(N_KV):
        out[k, :n_cmp, k * SEL_LANES:(k + 1) * SEL_LANES] = ov
    return jnp.asarray(out)


def _unselected_bias_table(ln):
    c = np.arange(ln)[:, None]
    r = np.arange(LANES)[None, :]
    out = np.zeros((N_KV, ln, LANES), np.float32)
    for k in range(N_KV):
        out[k] = np.where((r // SEL_LANES == k) & (r % SEL_LANES == c // SEL_BLOCK), MASK_BIAS, 0.0)
    return jnp.asarray(out, dtype=BF16)


def _rows_last(t):
    nd = t.ndim
    t = jnp.transpose(t, tuple(range(nd - 3)) + (nd - 2, nd - 1, nd - 3))
    return t.reshape(t.shape[:-3] + (KV_DIM, t.shape[-1]))


def _rows_first(t):
    nd = t.ndim
    t = t.reshape(t.shape[:-2] + (N_KV, HEAD_DIM, t.shape[-1]))
    return jnp.transpose(t, tuple(range(nd - 2)) + (nd, nd - 2, nd - 1))


def _compress_weights(pe, w1, w2):
    r = CMP_BLOCK // CMP_STRIDE
    flat = CMP_STRIDE * HEAD_DIM
    w1cat = jnp.concatenate([w1[m * CMP_STRIDE:(m + 1) * CMP_STRIDE].reshape(flat, CMP_HIDDEN) for m in range(r)], 1)
    pe2 = jnp.zeros((SUBLANES, flat), F32).at[:r].set(pe.reshape(r, flat))
    zeros = jnp.zeros_like(w2)
    return (pe2.astype(BF16), w1cat.astype(BF16), jnp.concatenate([w2, zeros], 1).astype(BF16),
            jnp.concatenate([zeros, w2], 1).astype(BF16))


def kernel(x_prompt, x_sample, state_conv, cache_cmp_k, cache_cmp_v, cache_sel_k, cache_sel_v, state_win_k, state_win_v, page_table, c_prompt, c_sample, ada_w, ada_b, norm_mix_g, norm_ffn_g, ffn_w1, ffn_w2, even_w_in, even_w_out, conv_w, conv_b, conv_ln_g, conv_ln_b, sgu_ln_g, sgu_ln_b, sgu_w, sgu_b, odd_w_in, odd_w_out, cmp_pe_k, cmp_w1_k, cmp_w2_k, cmp_pe_v, cmp_w1_v, cmp_w2_v, final_norm_g):
    depth = ada_w.shape[0]
    bn, ln, _ = x_prompt.shape
    sn = x_sample.shape[0]
    past = page_table.shape[1] * PAGE_SIZE

    ada = _ada_all(jnp.concatenate([c_prompt, c_sample], 0), ada_w, ada_b)
    ov = _overlap_tables()
    ov_t = jnp.swapaxes(ov, 1, 2)
    nege = _unselected_bias_table(ln)
    cos_p, sin_p = _rope_tables(jnp.arange(ln))
    cos_s, sin_s = _rope_tables(jnp.full((sn,), past))
    g_fin = final_norm_g.reshape(1, D_MODEL)
    row = lambda a: a.reshape(1, -1)
    pools = [_rows_last(c) for c in (cache_cmp_k, cache_cmp_v, cache_sel_k, cache_sel_v)]
    win_k_all, win_v_all = _rows_last(state_win_k), _rows_last(state_win_v)

    xp = x_prompt
    xs = x_sample.reshape(1, sn, D_MODEL)
    conv_p, conv_s, chv_p, chv_s = [], [], [], []
    cols_p, cols_s, win_s = [], [], []
    for l in range(depth):
        mods_p = ada[l, :, :bn].reshape(6, bn, 1, D_MODEL)
        mods_s = ada[l, :, bn:].reshape(6, 1, sn, D_MODEL)
        g_mix = row(norm_mix_g[l])
        if l % 2 == 0:
            e = l // 2
            w_in = even_w_in[e].astype(BF16)
            w_out = even_w_out[e].astype(BF16)
            cln = (row(conv_b[e]), row(conv_ln_g[e]), row(conv_ln_b[e]), row(sgu_ln_g[e]), row(sgu_ln_b[e]))
            yp, bp, vp = _even_prompt(xp, mods_p, g_mix, w_in, conv_w[e], *cln, sgu_w[e], sgu_b[e].T, tm=256)
            w00 = row(jnp.repeat(sgu_w[e, :, 0, 0], B_GROUP_DIM))
            b0 = row(jnp.repeat(sgu_b[e, :, 0], B_GROUP_DIM))
            ys, a_s, v_s = _even_sample(xs[0], mods_s[:, 0], g_mix, w_in, jnp.swapaxes(state_conv[e], 0, 1),
                                        conv_w[e], *cln, w00, b0)
            conv_p.append(bp)
            conv_s.append(jnp.concatenate([state_conv[e][:, 1:], a_s[:, None, :]], axis=1))
            chv_p.append(vp)
            chv_s.append(v_s[:, None, :])
            ys = ys[None]
        else:
            o = l // 2
            q_cols = np.concatenate([np.arange(h * HEAD_DIM, (h + 1) * HEAD_DIM) for h in Q_HEAD_ORDER])
            w_in = jnp.concatenate([odd_w_in[o][:, q_cols], odd_w_in[o][:, q_cols.size:]], axis=1)
            w_in = jnp.pad(w_in, ((0, 0), (0, ODD_IN_PAD - w_in.shape[-1]))).astype(BF16)
            w_out = odd_w_out[o].astype(BF16)
            cw_k = _compress_weights(cmp_pe_k[o], cmp_w1_k[o], cmp_w2_k[o])
            cw_v = _compress_weights(cmp_pe_v[o], cmp_w1_v[o], cmp_w2_v[o])
            qs, qrs, c0, c1, c2, c3, rowsb, cols, colsb, gates = _nsa_project(
                xp, mods_p, g_mix, w_in, cos_p, sin_p, tm=256, decode=False)
            kc, vc = _compress_prompt((c0, c1, c2, c3), cw_k, cw_v)
            yp = _attn_prompt(qs, qrs, gates, kc, vc, rowsb, colsb, ov_t, nege, t=256)
            cols_p.append(cols)
            qs, qrs, rows, cols, gates = _nsa_project(xs, mods_s, g_mix, w_in, cos_s, sin_s, tm=sn, decode=True)
            heads = lambda t: jnp.swapaxes(t[0], 0, 1)
            col = lambda i: rows[0, :, i * KV_DIM:(i + 1) * KV_DIM].reshape(sn, KV_DIM, 1)
            ys, wk_new, wv_new = _attn_sample(
                page_table, o, heads(qs), heads(qrs), gates.reshape(sn, 1, LANES), rows.reshape(sn, 1, 4 * KV_DIM),
                col(2), col(3), win_k_all, win_v_all, *pools, cw_k, cw_v, ov)
            ys = ys.reshape(1, sn, D_MODEL)
            cols_s.append(cols[0])
            win_s.append((wk_new, wv_new))
        final = l == depth - 1
        w1 = ffn_w1[l].astype(BF16)
        w2 = ffn_w2[l].astype(BF16)
        g_ffn = row(norm_ffn_g[l])
        xp = _post(xp, yp, mods_p, g_ffn, w_out, w1, w2, g_fin, final=final, tm=512)
        xs = _post(xs, ys, mods_s, g_ffn, w_out, w1, w2, g_fin, final=final, tm=sn)
    st = lambda lst: jnp.stack(lst, axis=0)
    nw = min(WINDOW, ln)
    stream_p = lambda i, lo: _rows_first(st([c[:, i * KV_DIM:(i + 1) * KV_DIM, lo:] for c in cols_p]))
    stream_s = lambda i: _rows_first(st([c[i * KV_DIM:(i + 1) * KV_DIM, :] for c in cols_s]))[:, :, None]
    return (xp, xs.reshape(sn, 1, D_MODEL),
            st(conv_p), st(conv_s), st(chv_p), st(chv_s),
            stream_p(0, 0), stream_p(1, 0), stream_p(2, 0), stream_p(3, 0), stream_p(4, ln - nw), stream_p(5, ln - nw),
            stream_s(0), stream_s(1), stream_s(2), stream_s(3),
            _rows_first(st([w[0] for w in win_s])), _rows_first(st([w[1] for w in win_s])))
```

```python
import functools

import numpy as np
import jax
import jax.numpy as jnp
from jax import lax
from jax.experimental import pallas as pl
from jax.experimental.pallas import tpu as pltpu

F32 = jnp.float32
BF16 = jnp.bfloat16
HIGHEST = lax.Precision.HIGHEST

LANES = 128
SUBLANES = 8
VMEM_LIMIT_BYTES = 56 * 1024 * 1024

D_MODEL = 1024
D_FF = 4 * D_MODEL
D_A = D_MODEL // 2
CONV_WIDTH = 31
CONV_HALO = 32
D_B = D_MODEL // 2
B_GROUPS = 4
B_GROUP_DIM = D_B // B_GROUPS
CHUNK = 128
N_HEADS = 16
HEAD_DIM = 64
N_KV = 4
GROUP = N_HEADS // N_KV
KV_DIM = N_KV * HEAD_DIM
N_PAIR = KV_DIM // LANES
CMP_BLOCK = 32
CMP_STRIDE = 16
CMP_HIDDEN = 2 * HEAD_DIM
SEL_BLOCK = 64
N_SEL = 8
SEL_LANES = 32
SEL_SHIFT = 6
SEL_LANES_SHIFT = 5
WINDOW = 512
PAGE_SIZE = 128
ROPE_THETA = 10000.0
EPS = 1e-6
NEG = -1e30
MASK_BIAS = -(2.0 ** 100)
FORCED = 1e4
SCALE = HEAD_DIM ** -0.5
ODD_IN_PAD = N_HEADS * HEAD_DIM + 6 * KV_DIM + LANES


def _cparams(*sem):
    return pltpu.CompilerParams(dimension_semantics=sem, vmem_limit_bytes=VMEM_LIMIT_BYTES)


def _const_spec(shape):
    n = len(shape)
    return pl.BlockSpec(shape, lambda *_: (0,) * n)


def _sigmoid(x):
    return 1.0 / (1.0 + jnp.exp(-x))


def _silu(x):
    return x * _sigmoid(x)


def _gelu(x):
    return 0.5 * x * (1.0 + jnp.tanh(np.sqrt(2.0 / np.pi).astype(np.float32) * (x + 0.044715 * (x * x * x))))


def _rmsnorm(x, g):
    return x * lax.rsqrt(jnp.mean(x * x, -1, keepdims=True) + EPS) * g


def _layernorm(x, g, b):
    xc = x - jnp.mean(x, -1, keepdims=True)
    return xc * lax.rsqrt(jnp.mean(xc * xc, -1, keepdims=True) + EPS) * g + b


def _modulate(x, g, shift, scale):
    return _rmsnorm(x, g) * (1.0 + scale) + shift


def _dot(a, b):
    return jnp.dot(a, b, preferred_element_type=F32)


def _dot_nt(a, b):
    return lax.dot_general(a, b, (((1,), (1,)), ((), ())), preferred_element_type=F32)


def _lane_iota(shape):
    return lax.broadcasted_iota(jnp.int32, shape, len(shape) - 1)


def _masked_softmax(s, mask):
    s = jnp.where(mask, s, NEG)
    m = jnp.max(s, -1, keepdims=True)
    p = jnp.where(mask, jnp.exp(s - m), 0.0)
    return p / jnp.maximum(jnp.sum(p, -1, keepdims=True), 1e-30)


def _rope_pair(xs, cos, sin_signed):
    half = HEAD_DIM // 2
    first = (_lane_iota(xs.shape) & (HEAD_DIM - 1)) < half
    swapped = jnp.where(first, pltpu.roll(xs, LANES - half, 1), pltpu.roll(xs, half, 1))
    return xs * cos + swapped * sin_signed


def _select_blocks(imp, pos, n_top):
    lane = _lane_iota(imp.shape)
    j = lane & (SEL_LANES - 1)
    cur = pos >> SEL_SHIFT
    valid = j * SEL_BLOCK <= pos
    forced = (j == 0) | (j == cur) | (j == cur - 1)
    score = jnp.where(valid, jnp.where(forced, FORCED, imp), NEG)
    rank = jnp.zeros(imp.shape, F32)
    for d in range(1, SEL_LANES):
        nonwrap = j >= d
        rot = jnp.where(nonwrap, pltpu.roll(score, d, 1), pltpu.roll(score, (d - SEL_LANES) % LANES, 1))
        ahead = (rot > score) | ((rot == score) & nonwrap)
        rank = rank + jnp.where(ahead, 1.0, 0.0)
    return jnp.where(rank < n_top, 1.0, 0.0)


def _select_blocks_t(imp_t, pos, n_top):
    row = lax.broadcasted_iota(jnp.int32, (LANES, 1), 0)
    j = row & (SEL_LANES - 1)
    cur = pos >> SEL_SHIFT
    valid = j * SEL_BLOCK <= pos
    forced = (j == 0) | (j == cur) | (j == cur - 1)
    score = jnp.where(valid, jnp.where(forced, FORCED, imp_t), NEG)
    jj = lax.broadcasted_iota(jnp.int32, (SEL_LANES, 1), 0)
    out = []
    for k in range(N_KV):
        sc = score[k * SEL_LANES:(k + 1) * SEL_LANES]
        rank = jnp.zeros(sc.shape, F32)
        for i in range(SEL_LANES):
            other = sc[i:i + 1]
            tie_first = jnp.where(jj > i, 1.0, 0.0)
            rank = rank + jnp.where(other > sc, 1.0, 0.0) + jnp.where(other == sc, tie_first, 0.0)
        out.append(jnp.where(rank < n_top, 1.0, 0.0))
    return jnp.concatenate(out, axis=0)


def _ada_body(c_ref, w_ref, b_ref, o_ref):
    a = _silu(c_ref[...])
    o_ref[0, 0] = jnp.dot(a, w_ref[0], preferred_element_type=F32, precision=HIGHEST) + b_ref[0, 0]


def _ada_all(c_all, ada_w, ada_b):
    depth = ada_w.shape[0]
    rows = c_all.shape[0]
    return pl.pallas_call(
        _ada_body,
        out_shape=jax.ShapeDtypeStruct((depth, 6, rows, D_MODEL), F32),
        grid=(depth, 6),
        in_specs=[
            _const_spec((rows, D_MODEL)),
            pl.BlockSpec((1, D_MODEL, D_MODEL), lambda l, j: (l, 0, j)),
            pl.BlockSpec((1, 1, 1, D_MODEL), lambda l, j: (l, j, 0, 0)),
        ],
        out_specs=pl.BlockSpec((1, 1, rows, D_MODEL), lambda l, j: (l, j, 0, 0)),
        compiler_params=_cparams("arbitrary", "arbitrary"),
        name="ada_params",
    )(c_all, ada_w, ada_b.reshape(depth, 6, 1, D_MODEL))


def _post_body(x_ref, y_ref, mod_ref, gffn_ref, wo_ref, w1_ref, w2_ref, gfin_ref, o_ref, *, final, ff_chunk):
    x = x_ref[0]
    x = x + mod_ref[2, 0] * _dot(y_ref[0], wo_ref[...])
    h = _modulate(x, gffn_ref[...], mod_ref[3, 0], mod_ref[4, 0]).astype(BF16)
    acc = jnp.zeros(x.shape, F32)
    for c in range(D_FF // ff_chunk):
        t = _dot(h, w1_ref[:, c * ff_chunk:(c + 1) * ff_chunk])
        t = jnp.square(jnp.maximum(t, 0.0)).astype(BF16)
        acc = acc + _dot(t, w2_ref[c * ff_chunk:(c + 1) * ff_chunk, :])
    x = x + mod_ref[5, 0] * acc
    if final:
        x = _rmsnorm(x, gfin_ref[...])
    o_ref[0] = x


def _post(x, y, mods, g_ffn, w_out, w1, w2, g_fin, *, final, tm):
    bn, ln, _ = x.shape
    d_in = y.shape[-1]
    r = mods.shape[2]
    tm = min(tm, ln)
    mod_rows = 1 if r == 1 else tm
    mod_map = (lambda b, t: (0, b, 0, 0)) if r == 1 else (lambda b, t: (0, b, t, 0))
    return pl.pallas_call(
        functools.partial(_post_body, final=final, ff_chunk=1024),
        out_shape=jax.ShapeDtypeStruct(x.shape, F32),
        grid=(bn, ln // tm),
        in_specs=[
            pl.BlockSpec((1, tm, D_MODEL), lambda b, t: (b, t, 0)),
            pl.BlockSpec((1, tm, d_in), lambda b, t: (b, t, 0)),
            pl.BlockSpec((6, 1, mod_rows, D_MODEL), mod_map),
            _const_spec((1, D_MODEL)),
            _const_spec((d_in, D_MODEL)),
            _const_spec((D_MODEL, D_FF)),
            _const_spec((D_FF, D_MODEL)),
            _const_spec((1, D_MODEL)),
        ],
        out_specs=pl.BlockSpec((1, tm, D_MODEL), lambda b, t: (b, t, 0)),
        compiler_params=_cparams("arbitrary", "arbitrary"),
        name="post_mlp",
    )(x, y, mods, g_ffn, w_out, w1, w2, g_fin)


def _even_prompt_body(x_ref, mod_ref, gmix_ref, win_ref, cw_ref, cb_ref, clg_ref, clb_ref, slg_ref, slb_ref,
                      sw_ref, sbt_ref, y_ref, conv_ref, chv_ref, ext_ref, sh_ref, *, tm, conv_rows):
    t = pl.program_id(1)
    h = _modulate(x_ref[0], gmix_ref[...], mod_ref[0, 0], mod_ref[1, 0]).astype(BF16)
    z = _dot(h, win_ref[...])
    a = z[:, :D_A] * _sigmoid(z[:, D_A:2 * D_A])
    u = _gelu(z[:, 2 * D_A:2 * D_A + D_B])
    v = _layernorm(_gelu(z[:, 2 * D_A + D_B:]), slg_ref[...], slb_ref[...])

    @pl.when(t == 0)
    def _():
        ext_ref[0:CONV_HALO, :] = jnp.zeros((CONV_HALO, D_A), F32)

    ext_ref[CONV_HALO:CONV_HALO + tm, :] = a
    first = CONV_HALO - (CONV_WIDTH - 1)
    span = sh_ref.shape[1]
    for sft in range(1, SUBLANES):
        sh_ref[sft - 1] = ext_ref[sft:sft + span, :]
    pieces = []
    for c in range(tm // conv_rows):
        acc = jnp.zeros((conv_rows, D_A), F32) + cb_ref[...]
        for w in range(CONV_WIDTH):
            sft = (first + w) % SUBLANES
            r0 = first + w - sft + c * conv_rows
            src = ext_ref[r0:r0 + conv_rows, :] if sft == 0 else sh_ref[sft - 1, r0:r0 + conv_rows, :]
            acc = acc + src * cw_ref[w:w + 1, :]
        pieces.append(acc)
    a_conv = jnp.concatenate(pieces, axis=0)
    conv_ref[0] = ext_ref[tm + first:tm + CONV_HALO, :]
    ext_ref[0:CONV_HALO, :] = ext_ref[tm:tm + CONV_HALO, :]
    a_out = _silu(_layernorm(a_conv, clg_ref[...], clb_ref[...]))

    row = lax.broadcasted_iota(jnp.int32, (CHUNK, CHUNK), 0)
    col = lax.broadcasted_iota(jnp.int32, (CHUNK, CHUNK), 1)
    wm = [jnp.where(col <= row, sw_ref[g], 0.0).astype(BF16) for g in range(B_GROUPS)]
    vb = v.astype(BF16)
    gate_rows = []
    for c in range(tm // CHUNK):
        cols = []
        for g in range(B_GROUPS):
            vg = vb[c * CHUNK:(c + 1) * CHUNK, g * B_GROUP_DIM:(g + 1) * B_GROUP_DIM]
            cols.append(_dot(wm[g], vg) + sbt_ref[:, g:g + 1])
        gate_rows.append(jnp.concatenate(cols, axis=1))
    b_out = u * jnp.concatenate(gate_rows, axis=0)

    y_ref[0, :, :D_A] = a_out.astype(BF16)
    y_ref[0, :, D_A:] = b_out.astype(BF16)
    chv_ref[0] = v[tm - CHUNK:, :]


def _even_prompt(x, mods, g_mix, w_in, conv_w, conv_b, cln_g, cln_b, sln_g, sln_b, sgu_w, sgu_bt, *, tm):
    bn, ln, _ = x.shape
    assert ln % tm == 0 and tm % CHUNK == 0 and ln >= CONV_HALO
    return pl.pallas_call(
        functools.partial(_even_prompt_body, tm=tm, conv_rows=64),
        out_shape=(
            jax.ShapeDtypeStruct((bn, ln, D_MODEL), BF16),
            jax.ShapeDtypeStruct((bn, CONV_WIDTH - 1, D_A), F32),
            jax.ShapeDtypeStruct((bn, CHUNK, D_B), F32),
        ),
        grid=(bn, ln // tm),
        in_specs=[
            pl.BlockSpec((1, tm, D_MODEL), lambda b, t: (b, t, 0)),
            pl.BlockSpec((6, 1, 1, D_MODEL), lambda b, t: (0, b, 0, 0)),
            _const_spec((1, D_MODEL)),
            _const_spec((D_MODEL, 2 * D_A + 2 * D_B)),
            _const_spec((CONV_WIDTH, D_A)),
            _const_spec((1, D_A)), _const_spec((1, D_A)), _const_spec((1, D_A)),
            _const_spec((1, D_B)), _const_spec((1, D_B)),
            _const_spec((B_GROUPS, CHUNK, CHUNK)),
            _const_spec((CHUNK, B_GROUPS)),
        ],
        out_specs=(
            pl.BlockSpec((1, tm, D_MODEL), lambda b, t: (b, t, 0)),
            pl.BlockSpec((1, CONV_WIDTH - 1, D_A), lambda b, t: (b, 0, 0)),
            pl.BlockSpec((1, CHUNK, D_B), lambda b, t: (b, 0, 0)),
        ),
        scratch_shapes=[pltpu.VMEM((tm + CONV_HALO, D_A), F32),
                        pltpu.VMEM((SUBLANES - 1, tm + CONV_HALO - SUBLANES, D_A), F32)],
        compiler_params=_cparams("arbitrary", "arbitrary"),
        name="even_prompt",
    )(x, mods, g_mix, w_in, conv_w, conv_b, cln_g, cln_b, sln_g, sln_b, sgu_w, sgu_bt)


def _even_sample_body(x_ref, mod_ref, gmix_ref, win_ref, st_ref, cw_ref, cb_ref, clg_ref, clb_ref, slg_ref,
                      slb_ref, w00_ref, b0_ref, y_ref, a_ref, v_ref):
    h = _modulate(x_ref[...], gmix_ref[...], mod_ref[0], mod_ref[1]).astype(BF16)
    z = _dot(h, win_ref[...])
    a = z[:, :D_A] * _sigmoid(z[:, D_A:2 * D_A])
    u = _gelu(z[:, 2 * D_A:2 * D_A + D_B])
    v = _layernorm(_gelu(z[:, 2 * D_A + D_B:]), slg_ref[...], slb_ref[...])
    acc = a * cw_ref[CONV_WIDTH - 1:CONV_WIDTH, :] + cb_ref[...]
    for w in range(CONV_WIDTH - 1):
        acc = acc + st_ref[w] * cw_ref[w:w + 1, :]
    a_out = _silu(_layernorm(acc, clg_ref[...], clb_ref[...]))
    b_out = u * (w00_ref[...] * v + b0_ref[...])
    y_ref[:, :D_A] = a_out.astype(BF16)
    y_ref[:, D_A:] = b_out.astype(BF16)
    a_ref[...] = a
    v_ref[...] = v


def _even_sample(x, mods, g_mix, w_in, state_t, conv_w, conv_b, cln_g, cln_b, sln_g, sln_b, w00, b0):
    sn = x.shape[0]
    shapes = [a.shape for a in (x, mods, g_mix, w_in, state_t, conv_w, conv_b, cln_g, cln_b, sln_g, sln_b, w00, b0)]
    return pl.pallas_call(
        _even_sample_body,
        out_shape=(
            jax.ShapeDtypeStruct((sn, D_MODEL), BF16),
            jax.ShapeDtypeStruct((sn, D_A), F32),
            jax.ShapeDtypeStruct((sn, D_B), F32),
        ),
        grid=(1,),
        in_specs=[_const_spec(s) for s in shapes],
        out_specs=(_const_spec((sn, D_MODEL)), _const_spec((sn, D_A)), _const_spec((sn, D_B))),
        compiler_params=_cparams("arbitrary"),
        name="even_sample",
    )(x, mods, g_mix, w_in, state_t, conv_w, conv_b, cln_g, cln_b, sln_g, sln_b, w00, b0)


def _proj_body(x_ref, mod_ref, gmix_ref, w_ref, cos_ref, sin_ref,
               qs_ref, qrs_ref, c0_ref, c1_ref, c2_ref, c3_ref, rows_ref, rowsb_ref, cols_ref, colsb_ref, g_ref):
    tm = x_ref.shape[1]
    h = _modulate(x_ref[0], gmix_ref[...], mod_ref[0, 0], mod_ref[1, 0]).astype(BF16)
    z = _dot(h, w_ref[...])
    cos = cos_ref[...]
    sin = sin_ref[...]
    q0 = N_HEADS * HEAD_DIM
    low = _lane_iota((tm, LANES)) < HEAD_DIM
    for c in range(q0 // LANES):
        xs = z[:, c * LANES:(c + 1) * LANES] * SCALE
        xr = _rope_pair(xs, cos, sin)
        for half in range(2):
            hd = 2 * c + half
            par = (hd // GROUP) % 2
            keep = low if par == 0 else jnp.logical_not(low)
            for src, ref in ((xs, qs_ref), (xr, qrs_ref)):
                v = src if half == par else pltpu.roll(src, HEAD_DIM, 1)
                ref[0, hd] = jnp.where(keep, v, 0.0).astype(BF16)

    def kv(i):
        return z[:, q0 + i * KV_DIM:q0 + (i + 1) * KV_DIM]

    def kv_rot(i):
        base = q0 + i * KV_DIM
        return jnp.concatenate(
            [_rope_pair(z[:, base + c * LANES:base + (c + 1) * LANES], cos, sin) for c in range(N_PAIR)], axis=1)

    rows = jnp.concatenate([kv(0), kv(1), kv_rot(2), kv(3), kv_rot(4), kv(5)], axis=1)
    for i, ref in enumerate((c0_ref, c1_ref, c2_ref, c3_ref)):
        ref[0] = rows[:, i * LANES:(i + 1) * LANES]
    rows_ref[0] = rows[:, 2 * KV_DIM:]
    rowsb_ref[0] = rows[:, 2 * KV_DIM:].astype(BF16)
    cols = rows.T
    cols_ref[0] = cols
    colsb_ref[0] = cols[2 * KV_DIM:].astype(BF16)
    g_ref[0] = _sigmoid(z[:, q0 + 6 * KV_DIM:])


def _nsa_project(x, mods, g_mix, w_in, cos, sin, *, tm):
    bn, ln, _ = x.shape
    r = mods.shape[2]
    tm = min(tm, ln)
    mod_rows = 1 if r == 1 else tm
    mod_map = (lambda b, t: (0, b, 0, 0)) if r == 1 else (lambda b, t: (0, b, t, 0))
    tok = lambda w: pl.BlockSpec((1, tm, w), lambda b, t: (b, t, 0))
    col = lambda w: pl.BlockSpec((1, w, tm), lambda b, t: (b, 0, t))
    head = pl.BlockSpec((1, N_HEADS, tm, LANES), lambda b, t: (b, 0, t, 0))
    return pl.pallas_call(
        _proj_body,
        out_shape=(
            jax.ShapeDtypeStruct((bn, N_HEADS, ln, LANES), BF16),
            jax.ShapeDtypeStruct((bn, N_HEADS, ln, LANES), BF16),
            jax.ShapeDtypeStruct((bn, ln, LANES), F32),
            jax.ShapeDtypeStruct((bn, ln, LANES), F32),
            jax.ShapeDtypeStruct((bn, ln, LANES), F32),
            jax.ShapeDtypeStruct((bn, ln, LANES), F32),
            jax.ShapeDtypeStruct((bn, ln, 4 * KV_DIM), F32),
            jax.ShapeDtypeStruct((bn, ln, 4 * KV_DIM), BF16),
            jax.ShapeDtypeStruct((bn, 6 * KV_DIM, ln), F32),
            jax.ShapeDtypeStruct((bn, 4 * KV_DIM, ln), BF16),
            jax.ShapeDtypeStruct((bn, ln, LANES), F32),
        ),
        grid=(bn, ln // tm),
        in_specs=[
            tok(D_MODEL),
            pl.BlockSpec((6, 1, mod_rows, D_MODEL), mod_map),
            _const_spec((1, D_MODEL)),
            _const_spec((D_MODEL, ODD_IN_PAD)),
            pl.BlockSpec((tm, LANES), lambda b, t: (t, 0)),
            pl.BlockSpec((tm, LANES), lambda b, t: (t, 0)),
        ],
        out_specs=(head, head, tok(LANES), tok(LANES), tok(LANES), tok(LANES), tok(4 * KV_DIM), tok(4 * KV_DIM),
                   col(6 * KV_DIM), col(4 * KV_DIM), tok(LANES)),
        compiler_params=_cparams("arbitrary", "arbitrary"),
        name="nsa_project",
    )(x, mods, g_mix, w_in, cos, sin)


def _compress_rows(load_rows, pe2_ref, w1_ref, w2e_ref, w2o_ref):
    n_chunk = 128
    lane = _lane_iota((n_chunk, LANES))
    low = lane < HEAD_DIM
    heads = [[] for _ in range(N_KV)]
    for p in range(CMP_STRIDE // 2):
        for m in range(N_PAIR):
            lo = load_rows(2 * p, m)
            hi = load_rows(2 * p + 1, m)
            heads[2 * m].append(jnp.where(low, lo, pltpu.roll(hi, HEAD_DIM, 1)))
            heads[2 * m + 1].append(jnp.where(low, pltpu.roll(lo, HEAD_DIM, 1), hi))
    x_all = jnp.concatenate([jnp.concatenate(hp, axis=1) for hp in heads], axis=0).astype(BF16)
    ab = _dot(x_all, w1_ref[...])
    pe = _dot(pe2_ref[...], w1_ref[...])
    hid0 = pe[0:1, :CMP_HIDDEN] + pe[1:2, CMP_HIDDEN:]
    hid = []
    for k in range(N_KV):
        a_part = ab[k * n_chunk:(k + 1) * n_chunk, :CMP_HIDDEN]
        b_part = ab[k * n_chunk:(k + 1) * n_chunk, CMP_HIDDEN:]
        nxt = pltpu.roll(b_part, n_chunk - 1, 0)
        hid.append(_gelu(a_part + nxt + hid0).astype(BF16))
    out = []
    for m in range(N_PAIR):
        out.append(_dot(hid[2 * m], w2e_ref[...]) + _dot(hid[2 * m + 1], w2o_ref[...]))
    return jnp.concatenate(out, axis=1)


def _compress_prompt_body(c0_ref, c1_ref, c2_ref, c3_ref, pek_ref, w1k_ref, w2ke_ref, w2ko_ref,
                          pev_ref, w1v_ref, w2ve_ref, w2vo_ref, kc_ref, vc_ref):
    slabs = (c0_ref, c1_ref, c2_ref, c3_ref)
    n_chunk = c0_ref.shape[1] // CMP_STRIDE
    assert n_chunk == 128

    def rows(first):
        return lambda j, m: slabs[first + m][0, pl.ds(j, n_chunk, stride=CMP_STRIDE), :]

    kc_ref[0] = _compress_rows(rows(0), pek_ref, w1k_ref, w2ke_ref, w2ko_ref).astype(BF16)
    vc_ref[0] = _compress_rows(rows(N_PAIR), pev_ref, w1v_ref, w2ve_ref, w2vo_ref).astype(BF16)


def _compress_prompt(slabs, cw_k, cw_v):
    bn, ln, _ = slabs[0].shape
    wspecs = [_const_spec(a.shape) for a in cw_k + cw_v]
    seq = pl.BlockSpec((1, ln, LANES), lambda b: (b, 0, 0))
    out = pl.BlockSpec((1, 128, KV_DIM), lambda b: (b, 0, 0))
    return pl.pallas_call(
        _compress_prompt_body,
        out_shape=(jax.ShapeDtypeStruct((bn, 128, KV_DIM), BF16),) * 2,
        grid=(bn,),
        in_specs=[seq] * 4 + wspecs,
        out_specs=(out, out),
        compiler_params=_cparams("arbitrary"),
        name="compress_prompt",
    )(*slabs, *cw_k, *cw_v)


ONES_ROWS = 16


def _flash_step(carry, s, v1):
    m_i, acc = carry
    m_new = jnp.maximum(m_i, jnp.max(s, 0, keepdims=True))
    alpha = jnp.exp(m_i - m_new)
    p = jnp.exp(s - m_new)
    return m_new, alpha * acc + _dot(v1, p.astype(BF16))


def _attn_prompt_body(qs_ref, qrs_ref, g_ref, kc_ref, vc_ref, krow_ref, kvt_ref, ovt_ref, nege_ref, o_ref, *, t):
    qi = pl.program_id(1)
    q0 = pl.multiple_of(qi * t, t)
    rows4 = GROUP * t
    pos_t = q0 + (_lane_iota((1, rows4)) & (t - 1))
    gates_t = g_ref[0].T

    def q_t(ref, k):
        return ref[0, GROUP * k:GROUP * (k + 1)].reshape(rows4, LANES).astype(F32).T.astype(BF16)

    def per_token(x):
        return jnp.concatenate([x] * GROUP, axis=1)

    key_n = lax.broadcasted_iota(jnp.int32, (LANES, 1), 0)
    cmask = (key_n < LANES - 1) & (key_n * CMP_STRIDE + (CMP_BLOCK - 1) <= pos_t)
    o_cmp = []
    imp_t = jnp.zeros((LANES, t), F32)
    for k in range(N_KV):
        m = k // 2
        s = jnp.where(cmask, _dot(kc_ref[0, :, m * LANES:(m + 1) * LANES], q_t(qs_ref, k)), NEG)
        p = jnp.where(cmask, jnp.exp(s - jnp.max(s, 0, keepdims=True)), 0.0)
        p = p / jnp.maximum(jnp.sum(p, 0, keepdims=True), 1e-30)
        o_cmp.append(_dot(vc_ref[0, :, m * LANES:(m + 1) * LANES].astype(F32).T.astype(BF16), p.astype(BF16)))
        psum = p[:, 0:t] + p[:, t:2 * t] + p[:, 2 * t:3 * t] + p[:, 3 * t:4 * t]
        imp_t = imp_t + jnp.dot(ovt_ref[k], psum, preferred_element_type=F32, precision=HIGHEST)
    unsel = (1.0 - _select_blocks_t(imp_t, pos_t[:, :t], N_SEL)).astype(BF16)
    ones = jnp.ones((ONES_ROWS, t), BF16)
    rows8 = 2 * rows4

    def per_pair(x):
        return jnp.concatenate([x] * (2 * GROUP), axis=1)

    kr = lax.broadcasted_iota(jnp.int32, (t, t), 0)
    qc = lax.broadcasted_iota(jnp.int32, (t, t), 1)
    bias_diag = per_pair(jnp.where(kr <= qc, 0.0, MASK_BIAS))
    bias_edge = per_pair(jnp.where(kr >= qc, 0.0, MASK_BIAS))
    seg = lax.broadcasted_iota(jnp.int32, (LANES, 1), 0) >> SEL_LANES_SHIFT

    def k_tile(stream, m, kt):
        k0 = pl.multiple_of(kt * t, t)
        base = stream * KV_DIM + m * LANES
        return krow_ref[0, pl.ds(k0, t), base:base + LANES]

    def v_ones(stream, m, kt):
        k0 = pl.multiple_of(kt * t, t)
        base = stream * KV_DIM + m * LANES
        return jnp.concatenate([kvt_ref[0, base:base + LANES, pl.ds(k0, t)], ones], axis=0)

    def init():
        return jnp.full((1, rows8), NEG, F32), jnp.zeros((LANES + ONES_ROWS, rows8), F32)

    def finish(carry):
        _, acc = carry
        return acc[:LANES] / jnp.maximum(acc[LANES:LANES + 1], 1e-30)

    heads = [None] * N_HEADS
    for m in range(N_PAIR):
        qr_t = jnp.concatenate([q_t(qrs_ref, 2 * m), q_t(qrs_ref, 2 * m + 1)], axis=1)
        zero = jnp.zeros(unsel.shape, BF16)
        own_blocks = [per_token(jnp.where(seg == 2 * m + kk, unsel, zero)) for kk in range(2)]

        qu_t = jnp.concatenate([qr_t, jnp.concatenate(own_blocks, axis=1)], axis=0)

        def sel_scores(kt, m=m, qu_t=qu_t):
            k0 = pl.multiple_of(kt * t, t)
            return _dot(jnp.concatenate([k_tile(0, m, kt), nege_ref[pl.ds(k0, t), :]], axis=1), qu_t)

        def sel_step(kt, cr, m=m, sel_scores=sel_scores):
            return _flash_step(cr, sel_scores(kt), v_ones(1, m, kt))

        def sel_pair(i, cr, sel_step=sel_step):
            return sel_step(2 * i + 1, sel_step(2 * i, cr))

        carry = lax.fori_loop(0, qi // 2, sel_pair, init())
        carry = lax.cond(qi % 2 == 1, lambda cr: sel_step(qi - 1, cr), lambda cr: cr, carry)
        carry = _flash_step(carry, sel_scores(qi) + bias_diag, v_ones(1, m, qi))
        o_sel = finish(carry)

        carry = init()
        for back, bias in ((2, bias_edge), (1, None)):
            def visit(cr, back=back, bias=bias, m=m, qr_t=qr_t):
                s = _dot(k_tile(2, m, qi - back), qr_t)
                return _flash_step(cr, s if bias is None else s + bias, v_ones(3, m, qi - back))
            carry = lax.cond(qi >= back, visit, lambda cr: cr, carry)
        carry = _flash_step(carry, _dot(k_tile(2, m, qi), qr_t) + bias_diag, v_ones(3, m, qi))
        o_win = finish(carry)

        for kk in range(2):
            k = 2 * m + kk
            dims = slice(kk * HEAD_DIM, (kk + 1) * HEAD_DIM)
            for g in range(GROUP):
                h = GROUP * k + g
                cols = slice(g * t, (g + 1) * t)
                wide = slice((kk * GROUP + g) * t, (kk * GROUP + g + 1) * t)
                heads[h] = (gates_t[3 * h:3 * h + 1] * o_cmp[k][dims, cols]
                            + gates_t[3 * h + 1:3 * h + 2] * o_sel[dims, wide]
                            + gates_t[3 * h + 2:3 * h + 3] * o_win[dims, wide])
    o_ref[0] = jnp.concatenate(heads, axis=0).T.astype(BF16)


def _attn_prompt(qs, qrs, gates, kc, vc, krow, kvt, ov_t, nege, *, t):
    bn, _, ln, _ = qs.shape
    assert ln % t == 0 and ln <= SEL_LANES * SEL_BLOCK and WINDOW == 2 * t and t & (t - 1) == 0
    head = pl.BlockSpec((1, N_HEADS, t, LANES), lambda b, i: (b, 0, i, 0))
    cmp_spec = pl.BlockSpec((1, 128, KV_DIM), lambda b, i: (b, 0, 0))
    return pl.pallas_call(
        functools.partial(_attn_prompt_body, t=t),
        out_shape=jax.ShapeDtypeStruct((bn, ln, D_MODEL), BF16),
        grid=(bn, ln // t),
        in_specs=[head, head, pl.BlockSpec((1, t, LANES), lambda b, i: (b, i, 0)), cmp_spec, cmp_spec,
                  pl.BlockSpec((1, ln, 4 * KV_DIM), lambda b, i: (b, 0, 0)),
                  pl.BlockSpec((1, 4 * KV_DIM, ln), lambda b, i: (b, 0, 0)),
                  _const_spec(ov_t.shape), _const_spec(nege.shape)],
        out_specs=pl.BlockSpec((1, t, D_MODEL), lambda b, i: (b, i, 0)),
        compiler_params=_cparams("arbitrary", "arbitrary"),
        name="attn_prompt",
    )(qs, qrs, gates, kc, vc, krow, kvt, ov_t, nege)


def _attn_sample_body(pt_ref, qs_ref, qrs_ref, g_ref, new_ref, wkc_ref, wvc_ref, wks_ref, wvs_ref, *rest, n_pages):
    del pt_ref
    pools = [rest[i * n_pages:(i + 1) * n_pages] for i in range(4)]
    (pek_ref, w1k_ref, w2ke_ref, w2ko_ref, pev_ref, w1v_ref, w2ve_ref, w2vo_ref, ov_ref, expand_ref,
     o_ref, wko_ref, wvo_ref, x_ref) = rest[4 * n_pages:]
    past = n_pages * PAGE_SIZE

    def compress(pages, *weights):
        pitch = CMP_STRIDE + 1
        for i, pg in enumerate(pages):
            for m in range(N_PAIR):
                rows_t = pg[0, 0, m * LANES:(m + 1) * LANES, :].T
                for c in range(PAGE_SIZE // CMP_STRIDE):
                    r0 = (i * (PAGE_SIZE // CMP_STRIDE) + c) * pitch
                    x_ref[m, r0:r0 + CMP_STRIDE, :] = rows_t[c * CMP_STRIDE:(c + 1) * CMP_STRIDE]
        load = lambda j, m: x_ref[m, pl.ds(j, past // CMP_STRIDE, stride=pitch), :]
        return _compress_rows(load, *weights).astype(BF16)

    kc = compress(pools[0], pek_ref, w1k_ref, w2ke_ref, w2ko_ref)
    vc = compress(pools[1], pev_ref, w1v_ref, w2ve_ref, w2vo_ref)

    lane = _lane_iota((SUBLANES, LANES))
    upper_rows = lax.broadcasted_iota(jnp.int32, (SUBLANES, LANES), 0) >= GROUP
    gates = g_ref[0]
    pos = jnp.full((SUBLANES, 1), past, jnp.int32)

    cmask = lane < LANES - 1
    o_cmp = []
    imp = jnp.zeros((SUBLANES, LANES), F32)
    for m in range(N_PAIR):
        qs = qs_ref[0, 2 * GROUP * m:2 * GROUP * (m + 1), :]
        p = _masked_softmax(_dot_nt(qs, kc[:, m * LANES:(m + 1) * LANES]), cmask)
        o_cmp.append(_dot(p.astype(BF16), vc[:, m * LANES:(m + 1) * LANES]))
        for kk in range(2):
            psum = jnp.sum(p[kk * GROUP:(kk + 1) * GROUP], axis=0, keepdims=True)
            imp = imp + jnp.dot(jnp.broadcast_to(psum, (SUBLANES, LANES)), ov_ref[2 * m + kk],
                                preferred_element_type=F32, precision=HIGHEST)
    sel = _select_blocks(imp, pos, N_SEL - 1)
    seg = lane >> SEL_LANES_SHIFT
    expand = expand_ref[...]

    def new_row(stream, m):
        base = stream * KV_DIM + m * LANES
        return new_ref[0, :, base:base + LANES]

    for m in range(N_PAIR):
        rows = slice(m * LANES, (m + 1) * LANES)
        qrs = qrs_ref[0, 2 * GROUP * m:2 * GROUP * (m + 1), :]
        qrf = qrs.astype(F32)
        want = jnp.where(upper_rows, 2 * m + 1, 2 * m)
        sel_rows = jnp.where(seg == want, sel, 0.0).astype(BF16)
        ok = _dot(sel_rows, expand) > 0.5
        s = jnp.concatenate([_dot(qrs, pg[0, 0, rows, :].astype(BF16)) for pg in pools[2]], axis=1)
        s_new = jnp.sum(qrf * new_row(0, m), -1, keepdims=True)
        s = jnp.where(ok, s, NEG)
        mx = jnp.maximum(jnp.max(s, -1, keepdims=True), s_new)
        p = jnp.where(ok, jnp.exp(s - mx), 0.0)
        p_new = jnp.exp(s_new - mx)
        denom = jnp.maximum(jnp.sum(p, -1, keepdims=True) + p_new, 1e-30)
        pb = p.astype(BF16)
        acc = p_new * new_row(1, m)
        for i, pg in enumerate(pools[3]):
            acc = acc + _dot_nt(pb[:, i * PAGE_SIZE:(i + 1) * PAGE_SIZE], pg[0, 0, rows, :].astype(BF16))
        o_sel = acc / denom
        sw = _dot(qrs, wks_ref[0, 0, rows, :].astype(BF16))
        sw_new = jnp.sum(qrf * new_row(2, m), -1, keepdims=True)
        mw = jnp.maximum(jnp.max(sw, -1, keepdims=True), sw_new)
        pw = jnp.exp(sw - mw)
        pw_new = jnp.exp(sw_new - mw)
        dw = jnp.maximum(jnp.sum(pw, -1, keepdims=True) + pw_new, 1e-30)
        o_win = (_dot_nt(pw.astype(BF16), wvs_ref[0, 0, rows, :].astype(BF16)) + pw_new * new_row(3, m)) / dw
        for kk in range(2):
            for c in range(GROUP // 2):
                halves = []
                for gg in range(2):
                    g = 2 * c + gg
                    h = GROUP * (2 * m + kk) + g
                    r = kk * GROUP + g
                    o = (gates[:, 3 * h:3 * h + 1] * o_cmp[m][r:r + 1] + gates[:, 3 * h + 1:3 * h + 2] * o_sel[r:r + 1]
                         + gates[:, 3 * h + 2:3 * h + 3] * o_win[r:r + 1])
                    if gg != kk:
                        o = pltpu.roll(jnp.broadcast_to(o, (SUBLANES, LANES)), HEAD_DIM, 1)[0:1]
                    halves.append(o)
                col = (GROUP * (2 * m + kk)) // 2 + c
                o_ref[0, :, col * LANES:(col + 1) * LANES] = jnp.where(
                    _lane_iota((1, LANES)) < HEAD_DIM, halves[0], halves[1]).astype(BF16)

    nbuf = wks_ref.shape[-1]
    last = _lane_iota((KV_DIM, nbuf)) == nbuf - 1
    wko_ref[0] = jnp.where(last, wkc_ref[0], pltpu.roll(wks_ref[0, 0], nbuf - 1, 1))
    wvo_ref[0] = jnp.where(last, wvc_ref[0], pltpu.roll(wvs_ref[0, 0], nbuf - 1, 1))


def _attn_sample(page_table, layer, qs, qrs, gates, new_rows, wk_col, wv_col, win_k, win_v,
                 pool_ck, pool_cv, pool_sk, pool_sv, cw_k, cw_v, ov):
    sn = qs.shape[0]
    n_pages = page_table.shape[1]
    nbuf = win_k.shape[-1]
    past = n_pages * PAGE_SIZE
    assert nbuf == WINDOW and past == SEL_LANES * SEL_BLOCK
    per_seq = lambda *blk: pl.BlockSpec((1,) + blk, lambda s, pt: (s,) + (0,) * len(blk))
    page_map = lambda s, pt, p: (layer, pt[s, p], 0, 0)
    page_specs = [pl.BlockSpec((1, 1, KV_DIM, PAGE_SIZE), functools.partial(page_map, p=p)) for p in range(n_pages)]
    win_spec = pl.BlockSpec((1, 1, KV_DIM, nbuf), lambda s, pt: (layer, s, 0, 0))
    expand = (np.arange(LANES)[:, None] % SEL_LANES) == (np.arange(past)[None, :] // SEL_BLOCK)
    weights = cw_k + cw_v + (ov, jnp.asarray(expand, dtype=BF16))
    grid_spec = pltpu.PrefetchScalarGridSpec(
        num_scalar_prefetch=1,
        grid=(sn,),
        in_specs=[per_seq(N_HEADS, LANES), per_seq(N_HEADS, LANES), per_seq(1, LANES), per_seq(1, 4 * KV_DIM),
                  per_seq(KV_DIM, 1), per_seq(KV_DIM, 1), win_spec, win_spec]
        + page_specs * 4
        + [pl.BlockSpec(w.shape, functools.partial(lambda s, pt, n: (0,) * n, n=w.ndim)) for w in weights],
        out_specs=(per_seq(1, D_MODEL), per_seq(KV_DIM, nbuf), per_seq(KV_DIM, nbuf)),
        scratch_shapes=[pltpu.VMEM((N_PAIR, past // CMP_STRIDE * (CMP_STRIDE + 1) + SUBLANES - 1, LANES), F32)],
    )
    pools = [pool_ck] * n_pages + [pool_cv] * n_pages + [pool_sk] * n_pages + [pool_sv] * n_pages
    return pl.pallas_call(
        functools.partial(_attn_sample_body, n_pages=n_pages),
        out_shape=(jax.ShapeDtypeStruct((sn, 1, D_MODEL), BF16),
                   jax.ShapeDtypeStruct((sn, KV_DIM, nbuf), F32),
                   jax.ShapeDtypeStruct((sn, KV_DIM, nbuf), F32)),
        grid_spec=grid_spec,
        compiler_params=_cparams("arbitrary"),
        name="attn_sample",
    )(page_table, qs, qrs, gates, new_rows, wk_col, wv_col, win_k, win_v, *pools, *weights)


def _rope_tables(pos):
    half = HEAD_DIM // 2
    inv = jnp.power(jnp.float32(ROPE_THETA), -jnp.arange(half, dtype=F32) * (2.0 / HEAD_DIM))
    ang = pos.astype(F32)[:, None] * inv[None, :]
    cos, sin = jnp.cos(ang), jnp.sin(ang)
    return jnp.concatenate([cos, cos, cos, cos], -1), jnp.concatenate([-sin, sin, -sin, sin], -1)


def _overlap_tables():
    n_cmp, n_sel = LANES - 1, SEL_LANES
    i = np.arange(n_cmp)[:, None] * CMP_STRIDE
    j = np.arange(n_sel)[None, :] * SEL_BLOCK
    ov = ((i < j + SEL_BLOCK) & (i + CMP_BLOCK > j)).astype(np.float32)
    out = np.zeros((N_KV, LANES, LANES), np.float32)
    for k in range(N_KV):
        out[k, :n_cmp, k * SEL_LANES:(k + 1) * SEL_LANES] = ov
    return jnp.asarray(out)


def _unselected_bias_table(ln):
    c = np.arange(ln)[:, None]
    r = np.arange(LANES)[None, :]
    return jnp.asarray(np.where(r % SEL_LANES == c // SEL_BLOCK, MASK_BIAS, 0.0), dtype=BF16)


def _rows_last(t):
    nd = t.ndim
    t = jnp.transpose(t, tuple(range(nd - 3)) + (nd - 2, nd - 1, nd - 3))
    return t.reshape(t.shape[:-3] + (KV_DIM, t.shape[-1]))


def _rows_first(t):
    nd = t.ndim
    t = t.reshape(t.shape[:-2] + (N_KV, HEAD_DIM, t.shape[-1]))
    return jnp.transpose(t, tuple(range(nd - 2)) + (nd, nd - 2, nd - 1))


def _compress_weights(pe, w1, w2):
    r = CMP_BLOCK // CMP_STRIDE
    flat = CMP_STRIDE * HEAD_DIM
    w1cat = jnp.concatenate([w1[m * CMP_STRIDE:(m + 1) * CMP_STRIDE].reshape(flat, CMP_HIDDEN) for m in range(r)], 1)
    pe2 = jnp.zeros((SUBLANES, flat), F32).at[:r].set(pe.reshape(r, flat))
    zeros = jnp.zeros_like(w2)
    return (pe2.astype(BF16), w1cat.astype(BF16), jnp.concatenate([w2, zeros], 1).astype(BF16),
            jnp.concatenate([zeros, w2], 1).astype(BF16))


def kernel(x_prompt, x_sample, state_conv, cache_cmp_k, cache_cmp_v, cache_sel_k, cache_sel_v, state_win_k, state_win_v, page_table, c_prompt, c_sample, ada_w, ada_b, norm_mix_g, norm_ffn_g, ffn_w1, ffn_w2, even_w_in, even_w_out, conv_w, conv_b, conv_ln_g, conv_ln_b, sgu_ln_g, sgu_ln_b, sgu_w, sgu_b, odd_w_in, odd_w_out, cmp_pe_k, cmp_w1_k, cmp_w2_k, cmp_pe_v, cmp_w1_v, cmp_w2_v, final_norm_g):
    depth = ada_w.shape[0]
    bn, ln, _ = x_prompt.shape
    sn = x_sample.shape[0]
    past = page_table.shape[1] * PAGE_SIZE

    ada = _ada_all(jnp.concatenate([c_prompt, c_sample], 0), ada_w, ada_b)
    ov = _overlap_tables()
    ov_t = jnp.swapaxes(ov, 1, 2)
    nege = _unselected_bias_table(ln)
    cos_p, sin_p = _rope_tables(jnp.arange(ln))
    cos_s, sin_s = _rope_tables(jnp.full((sn,), past))
    g_fin = final_norm_g.reshape(1, D_MODEL)
    row = lambda a: a.reshape(1, -1)
    pools = [_rows_last(c) for c in (cache_cmp_k, cache_cmp_v, cache_sel_k, cache_sel_v)]
    win_k_all, win_v_all = _rows_last(state_win_k), _rows_last(state_win_v)

    xp = x_prompt
    xs = x_sample.reshape(1, sn, D_MODEL)
    conv_p, conv_s, chv_p, chv_s = [], [], [], []
    cols_p, cols_s, win_s = [], [], []
    for l in range(depth):
        mods_p = ada[l, :, :bn].reshape(6, bn, 1, D_MODEL)
        mods_s = ada[l, :, bn:].reshape(6, 1, sn, D_MODEL)
        g_mix = row(norm_mix_g[l])
        if l % 2 == 0:
            e = l // 2
            w_in = even_w_in[e].astype(BF16)
            w_out = even_w_out[e].astype(BF16)
            cln = (row(conv_b[e]), row(conv_ln_g[e]), row(conv_ln_b[e]), row(sgu_ln_g[e]), row(sgu_ln_b[e]))
            yp, bp, vp = _even_prompt(xp, mods_p, g_mix, w_in, conv_w[e], *cln, sgu_w[e], sgu_b[e].T, tm=256)
            w00 = row(jnp.repeat(sgu_w[e, :, 0, 0], B_GROUP_DIM))
            b0 = row(jnp.repeat(sgu_b[e, :, 0], B_GROUP_DIM))
            ys, a_s, v_s = _even_sample(xs[0], mods_s[:, 0], g_mix, w_in, jnp.swapaxes(state_conv[e], 0, 1),
                                        conv_w[e], *cln, w00, b0)
            conv_p.append(bp)
            conv_s.append(jnp.concatenate([state_conv[e][:, 1:], a_s[:, None, :]], axis=1))
            chv_p.append(vp)
            chv_s.append(v_s[:, None, :])
            ys = ys[None]
        else:
            o = l // 2
            w_in = jnp.pad(odd_w_in[o], ((0, 0), (0, ODD_IN_PAD - odd_w_in.shape[-1]))).astype(BF16)
            w_out = odd_w_out[o].astype(BF16)
            cw_k = _compress_weights(cmp_pe_k[o], cmp_w1_k[o], cmp_w2_k[o])
            cw_v = _compress_weights(cmp_pe_v[o], cmp_w1_v[o], cmp_w2_v[o])
            qs, qrs, c0, c1, c2, c3, _, rowsb, cols, colsb, gates = _nsa_project(
                xp, mods_p, g_mix, w_in, cos_p, sin_p, tm=256)
            kc, vc = _compress_prompt((c0, c1, c2, c3), cw_k, cw_v)
            yp = _attn_prompt(qs, qrs, gates, kc, vc, rowsb, colsb, ov_t, nege, t=256)
            cols_p.append(cols)
            qs, qrs, _, _, _, _, rows, _, cols, _, gates = _nsa_project(xs, mods_s, g_mix, w_in, cos_s, sin_s, tm=sn)
            heads = lambda t: jnp.swapaxes(t[0], 0, 1)
            col = lambda i: rows[0, :, i * KV_DIM:(i + 1) * KV_DIM].reshape(sn, KV_DIM, 1)
            ys, wk_new, wv_new = _attn_sample(
                page_table, o, heads(qs), heads(qrs), gates.reshape(sn, 1, LANES), rows.reshape(sn, 1, 4 * KV_DIM),
                col(2), col(3), win_k_all, win_v_all, *pools, cw_k, cw_v, ov)
            ys = ys.reshape(1, sn, D_MODEL)
            cols_s.append(cols[0])
            win_s.append((wk_new, wv_new))
        final = l == depth - 1
        w1 = ffn_w1[l].astype(BF16)
        w2 = ffn_w2[l].astype(BF16)
        g_ffn = row(norm_ffn_g[l])
        xp = _post(xp, yp, mods_p, g_ffn, w_out, w1, w2, g_fin, final=final, tm=512)
        xs = _post(xs, ys, mods_s, g_ffn, w_out, w1, w2, g_fin, final=final, tm=sn)
    st = lambda lst: jnp.stack(lst, axis=0)
    nw = min(WINDOW, ln)
    cols_p = st(cols_p)
    stream_p = lambda i, lo: _rows_first(cols_p[:, :, i * KV_DIM:(i + 1) * KV_DIM, lo:])
    cols_s = st(cols_s)
    stream_s = lambda i: _rows_first(cols_s[:, i * KV_DIM:(i + 1) * KV_DIM, :])[:, :, None]
    return (xp, xs.reshape(sn, 1, D_MODEL),
            st(conv_p), st(conv_s), st(chv_p), st(chv_s),
            stream_p(0, 0), stream_p(1, 0), stream_p(2, 0), stream_p(3, 0), stream_p(4, ln - nw), stream_p(5, ln - nw),
            stream_s(0), stream_s(1), stream_s(2), stream_s(3),
            _rows_first(st([w[0] for w in win_s])), _rows_first(st([w[1] for w in win_s])))
```
